```python
import jax, jax.numpy as jnp
from jax import lax
import numpy as np

D_MODEL = 1024
BATCH = 32
SEQ = 2048
DEPTH = 2
DEC_BATCH = 32
DEC_SEQ = 16
PAST_LEN = 4096

CHUNK = 64
A_HEADS = 16
A_HEAD_DIM = 64
A_WIDTH = A_HEADS * A_HEAD_DIM
A_DECAY_RANK = 64
A_ICLR_RANK = 64
A_GATE_RANK = 128
A_COLS = 3 * A_WIDTH + A_DECAY_RANK + A_ICLR_RANK + A_GATE_RANK
A_GN_EPS = 64e-5
B_HEADS = 4
B_KEY_DIM = 256
B_VAL_DIM = 512
B_QK_WIDTH = B_HEADS * B_KEY_DIM
B_V_WIDTH = B_HEADS * B_VAL_DIM
B_COLS = 2 * B_QK_WIDTH + 2 * B_V_WIDTH
GATE_COLS = 2 * D_MODEL
IN_COLS = A_COLS + B_COLS + GATE_COLS
FFN_HIDDEN = 4 * D_MODEL
ROPE_BASE = 10000.0
NORM_EPS = 1e-6

kernel_name = 'hybrid_rwkv7_retention_streaming_encoder_step'


def _rms_norm(x, g):
    xf = x.astype(jnp.float32)
    y = xf * lax.rsqrt(jnp.mean(xf * xf, axis=-1, keepdims=True) + NORM_EPS)
    return (y * g.astype(jnp.float32)).astype(x.dtype)


def _rope(x, pos):
    half = x.shape[-1] // 2
    inv_freq = ROPE_BASE ** (-jnp.linspace(0.0, 1.0, half, dtype=jnp.float32))
    ang = pos.astype(jnp.float32)[:, None] * inv_freq[None, :]
    cos = jnp.cos(ang)[None, :, None, :]
    sin = jnp.sin(ang)[None, :, None, :]
    x1, x2 = x[..., :half], x[..., half:]
    return jnp.concatenate([x1 * cos - x2 * sin, x1 * sin + x2 * cos], axis=-1)


def _rwkv7_step(S, inp):
    r, w, k, v, a, b = inp
    sa = jnp.einsum('bhvk,bhk->bhv', S, a)
    S = S * w[:, :, None, :] + sa[..., None] * b[:, :, None, :] + v[..., None] * k[:, :, None, :]
    y = jnp.einsum('bhvk,bhk->bhv', S, r)
    return S, y


def _rwkv7_branch(za, s0, p):
    bsz, t, _ = za.shape
    za = za.astype(jnp.float32)
    r, k, v, zw, zi, zg = jnp.split(
        za, [A_WIDTH, 2 * A_WIDTH, 3 * A_WIDTH, 3 * A_WIDTH + A_DECAY_RANK,
             3 * A_WIDTH + A_DECAY_RANK + A_ICLR_RANK], axis=-1)
    w_log = -jax.nn.softplus(-(p['a_w0'] + jnp.tanh(zw) @ p['a_w2'])) - 0.5
    decay = jnp.exp(-jnp.exp(w_log))
    iclr = jax.nn.sigmoid(p['a_a0'] + zi @ p['a_a2'])
    gate = jax.nn.sigmoid(zg) @ p['a_g2']
    kk = (k * p['a_kk']).reshape(bsz, t, A_HEADS, A_HEAD_DIM)
    kk = kk / jnp.maximum(jnp.sqrt(jnp.sum(kk * kk, axis=-1, keepdims=True)), 1e-12)
    k = k * (1.0 + (iclr - 1.0) * p['a_ka'])
    r_h = r.reshape(bsz, t, A_HEADS, A_HEAD_DIM)
    k_h = k.reshape(bsz, t, A_HEADS, A_HEAD_DIM)
    v_h = v.reshape(bsz, t, A_HEADS, A_HEAD_DIM)
    w_h = decay.reshape(bsz, t, A_HEADS, A_HEAD_DIM)
    i_h = iclr.reshape(bsz, t, A_HEADS, A_HEAD_DIM)
    xs = tuple(jnp.moveaxis(u, 1, 0) for u in (r_h, w_h, k_h, v_h, -kk, kk * i_h))
    s_new, y = lax.scan(_rwkv7_step, s0.astype(jnp.float32), xs)
    y = jnp.moveaxis(y, 0, 1)
    mu = jnp.mean(y, axis=-1, keepdims=True)
    var = jnp.mean(jnp.square(y - mu), axis=-1, keepdims=True)
    y = (y - mu) * lax.rsqrt(var + A_GN_EPS)
    y = y * p['a_ln_w'].reshape(A_HEADS, A_HEAD_DIM) + p['a_ln_b'].reshape(A_HEADS, A_HEAD_DIM)
    bonus = jnp.sum(r_h * k_h * p['a_rk'], axis=-1, keepdims=True) * v_h
    out = (y + bonus).reshape(bsz, t, A_WIDTH) * gate
    return out, s_new


def _retention_log_decay():
    return jnp.log(1.0 - 2.0 ** (-5.0 - jnp.arange(B_HEADS, dtype=jnp.float32)))


def _retention_chunk(S, inp):
    q, k, v = inp
    c = q.shape[2]
    log_g = _retention_log_decay()
    idx = jnp.arange(c, dtype=jnp.float32)
    diff = idx[:, None] - idx[None, :]
    mask = jnp.where(diff >= 0.0, jnp.exp(jnp.maximum(diff, 0.0)[None] * log_g[:, None, None]), 0.0)
    scores = jnp.einsum('bhid,bhjd->bhij', q, k) * mask
    inner = jnp.einsum('bhij,bhjv->bhiv', scores, v)
    q_dec = q * jnp.exp((idx + 1.0)[None, :] * log_g[:, None])[None, :, :, None]
    cross = jnp.einsum('bhid,bhdv->bhiv', q_dec, S)
    k_dec = k * jnp.exp((c - 1.0 - idx)[None, :] * log_g[:, None])[None, :, :, None]
    S = S * jnp.exp(c * log_g)[None, :, None, None] + jnp.einsum('bhjd,bhjv->bhdv', k_dec, v)
    return S, inner + cross


def _retention_branch(zb, s0, pos):
    bsz, t, _ = zb.shape
    zb = zb.astype(jnp.float32)
    q, k, v, g = jnp.split(zb, [B_QK_WIDTH, 2 * B_QK_WIDTH, 2 * B_QK_WIDTH + B_V_WIDTH], axis=-1)
    q = _rope(q.reshape(bsz, t, B_HEADS, B_KEY_DIM), pos)
    k = _rope(k.reshape(bsz, t, B_HEADS, B_KEY_DIM), pos) * (B_KEY_DIM ** -0.5)
    v = v.reshape(bsz, t, B_HEADS, B_VAL_DIM)
    c = CHUNK if t % CHUNK == 0 else t
    nc = t // c

    def blocks(u):
        return jnp.transpose(u.reshape(bsz, nc, c, B_HEADS, u.shape[-1]), (1, 0, 3, 2, 4))

    s_new, y = lax.scan(_retention_chunk, s0.astype(jnp.float32), (blocks(q), blocks(k), blocks(v)))
    y = jnp.transpose(y, (1, 0, 3, 2, 4)).reshape(bsz, t, B_HEADS, B_VAL_DIM)
    y = y * lax.rsqrt(jnp.mean(y * y, axis=-1, keepdims=True) + NORM_EPS)
    out = jax.nn.silu(g) * y.reshape(bsz, t, B_V_WIDTH)
    return out, s_new


def _token_mixer(h, shift_prev, s_rwkv, s_ret, pos, p):
    bsz, t, _ = h.shape
    z = h @ p['w_in']
    za, zb, zg = jnp.split(z, [A_COLS, A_COLS + B_COLS], axis=-1)
    za_prev = jnp.concatenate([shift_prev[:, None, :].astype(za.dtype), za[:, :-1]], axis=1)
    za_mix = za + (za_prev - za) * p['a_mu']
    y_a, new_rwkv = _rwkv7_branch(za_mix, s_rwkv, p)
    y_b, new_ret = _retention_branch(zb, s_ret, pos)
    gates = jax.nn.sigmoid(zg.astype(jnp.float32)).reshape(bsz, t, 2, D_MODEL)
    merged = (gates[:, :, 0] * (y_a.astype(h.dtype) @ p['w_branch_a'])
              + gates[:, :, 1] * (y_b.astype(h.dtype) @ p['w_branch_b']))
    out = merged.astype(h.dtype) @ p['w_out']
    return out, za[:, -1], new_rwkv, new_ret


def _layer(x, c, shift_prev, s_rwkv, s_ret, pos, p):
    mod = jax.nn.silu(c) @ p['ada_w'] + p['ada_b']
    sh1, sc1, gt1, sh2, sc2, gt2 = [u[:, None, :] for u in jnp.split(mod, 6, axis=-1)]
    h = (_rms_norm(x, p['norm_pre'][0]) * (1.0 + sc1) + sh1).astype(x.dtype)
    mix, new_shift, new_rwkv, new_ret = _token_mixer(h, shift_prev, s_rwkv, s_ret, pos, p)
    x = (x + gt1 * _rms_norm(mix, p['norm_post'][0])).astype(x.dtype)
    h = (_rms_norm(x, p['norm_pre'][1]) * (1.0 + sc2) + sh2).astype(x.dtype)
    f = jnp.square(jax.nn.relu(h @ p['w_ffn_up'])) @ p['w_ffn_down']
    x = (x + gt2 * _rms_norm(f, p['norm_post'][1])).astype(x.dtype)
    return x, new_shift, new_rwkv, new_ret


def setup_inputs(seed: int = 0) -> dict:
    key = jax.random.key(seed)
    ks = jax.random.split(key, 32)
    f32 = jnp.float32

    def nrm(i, shape, scale):
        return jax.random.normal(ks[i], shape, f32) * scale

    return {
        'x_prompt': nrm(0, (BATCH, SEQ, D_MODEL), 1.0),
        'x_sample': nrm(1, (DEC_BATCH, DEC_SEQ, D_MODEL), 1.0),
        'state_shift': nrm(2, (DEPTH, DEC_BATCH, A_COLS), 1.0),
        'state_rwkv': nrm(3, (DEPTH, DEC_BATCH, A_HEADS, A_HEAD_DIM, A_HEAD_DIM), 0.3),
        'state_ret': nrm(4, (DEPTH, DEC_BATCH, B_HEADS, B_KEY_DIM, B_VAL_DIM), 0.3),
        'c_prompt': nrm(5, (BATCH, D_MODEL), 1.0),
        'c_sample': nrm(6, (DEC_BATCH, D_MODEL), 1.0),
        'ada_w': nrm(7, (DEPTH, D_MODEL, 6 * D_MODEL), 0.5 * D_MODEL ** -0.5),
        'ada_b': nrm(8, (DEPTH, 6 * D_MODEL), 0.02),
        'norm_pre': 1.0 + nrm(9, (DEPTH, 2, D_MODEL), 0.02),
        'norm_post': 1.0 + nrm(10, (DEPTH, 2, D_MODEL), 0.02),
        'w_in': nrm(11, (DEPTH, D_MODEL, IN_COLS), D_MODEL ** -0.5),
        'a_mu': jax.random.uniform(ks[12], (DEPTH, A_COLS), f32, 0.0, 1.0),
        'a_w0': jax.random.uniform(ks[13], (DEPTH, A_WIDTH), f32, -6.0, -1.0),
        'a_w2': nrm(14, (DEPTH, A_DECAY_RANK, A_WIDTH), 0.1),
        'a_a0': nrm(15, (DEPTH, A_WIDTH), 0.1),
        'a_a2': nrm(16, (DEPTH, A_ICLR_RANK, A_WIDTH), 0.5 * A_ICLR_RANK ** -0.5),
        'a_g2': nrm(17, (DEPTH, A_GATE_RANK, A_WIDTH), A_GATE_RANK ** -0.5),
        'a_kk': 0.85 + nrm(18, (DEPTH, A_WIDTH), 0.05),
        'a_ka': 1.0 + nrm(19, (DEPTH, A_WIDTH), 0.05),
        'a_rk': nrm(20, (DEPTH, A_HEADS, A_HEAD_DIM), 0.1),
        'a_ln_w': 1.0 + nrm(21, (DEPTH, A_WIDTH), 0.02),
        'a_ln_b': nrm(22, (DEPTH, A_WIDTH), 0.02),
        'w_branch_a': nrm(23, (DEPTH, A_WIDTH, D_MODEL), A_WIDTH ** -0.5),
        'w_branch_b': nrm(24, (DEPTH, B_V_WIDTH, D_MODEL), B_V_WIDTH ** -0.5),
        'w_out': nrm(25, (DEPTH, D_MODEL, D_MODEL), D_MODEL ** -0.5),
        'w_ffn_up': nrm(26, (DEPTH, D_MODEL, FFN_HIDDEN), D_MODEL ** -0.5),
        'w_ffn_down': nrm(27, (DEPTH, FFN_HIDDEN, D_MODEL), FFN_HIDDEN ** -0.5),
    }


def reference(x_prompt, x_sample, state_shift, state_rwkv, state_ret, c_prompt, c_sample,
              ada_w, ada_b, norm_pre, norm_post, w_in, a_mu, a_w0, a_w2, a_a0, a_a2, a_g2,
              a_kk, a_ka, a_rk, a_ln_w, a_ln_b, w_branch_a, w_branch_b, w_out, w_ffn_up, w_ffn_down):
    t_p = x_prompt.shape[1]
    t_s = x_sample.shape[1]
    bp = x_prompt.shape[0]
    pos_p = jnp.arange(t_p, dtype=jnp.int32)
    pos_s = PAST_LEN + jnp.arange(t_s, dtype=jnp.int32)
    zero_shift = jnp.zeros((bp, A_COLS), x_prompt.dtype)
    zero_rwkv = jnp.zeros((bp, A_HEADS, A_HEAD_DIM, A_HEAD_DIM), jnp.float32)
    zero_ret = jnp.zeros((bp, B_HEADS, B_KEY_DIM, B_VAL_DIM), jnp.float32)

    xp = x_prompt
    xs = x_sample
    shift_p, rwkv_p, ret_p = [], [], []
    shift_s, rwkv_s, ret_s = [], [], []
    for l in range(DEPTH):
        p = {
            'ada_w': ada_w[l], 'ada_b': ada_b[l], 'norm_pre': norm_pre[l], 'norm_post': norm_post[l],
            'w_in': w_in[l], 'a_mu': a_mu[l], 'a_w0': a_w0[l], 'a_w2': a_w2[l], 'a_a0': a_a0[l],
            'a_a2': a_a2[l], 'a_g2': a_g2[l], 'a_kk': a_kk[l], 'a_ka': a_ka[l], 'a_rk': a_rk[l],
            'a_ln_w': a_ln_w[l], 'a_ln_b': a_ln_b[l], 'w_branch_a': w_branch_a[l],
            'w_branch_b': w_branch_b[l], 'w_out': w_out[l], 'w_ffn_up': w_ffn_up[l],
            'w_ffn_down': w_ffn_down[l],
        }
        xp, sp, rp, tp = _layer(xp, c_prompt, zero_shift, zero_rwkv, zero_ret, pos_p, p)
        xs, ss, rs, ts = _layer(xs, c_sample, state_shift[l], state_rwkv[l], state_ret[l], pos_s, p)
        shift_p.append(sp)
        rwkv_p.append(rp)
        ret_p.append(tp)
        shift_s.append(ss)
        rwkv_s.append(rs)
        ret_s.append(ts)

    new_shift_prompt = jnp.stack(shift_p).astype(state_shift.dtype)
    new_rwkv_prompt = jnp.stack(rwkv_p).astype(state_rwkv.dtype)
    new_ret_prompt = jnp.stack(ret_p).astype(state_ret.dtype)
    new_shift_sample = jnp.stack(shift_s).astype(state_shift.dtype)
    new_rwkv_sample = jnp.stack(rwkv_s).astype(state_rwkv.dtype)
    new_ret_sample = jnp.stack(ret_s).astype(state_ret.dtype)
    return (xp, xs, new_shift_prompt, new_rwkv_prompt, new_ret_prompt,
            new_shift_sample, new_rwkv_sample, new_ret_sample)
```

```python
import functools
import math

import jax
import jax.numpy as jnp
from jax import lax
from jax.experimental import pallas as pl
from jax.experimental.pallas import tpu as pltpu

F32 = jnp.float32
BF16 = jnp.bfloat16

D_MODEL = 1024
A_HEADS = 16
A_HEAD_DIM = 64
A_WIDTH = A_HEADS * A_HEAD_DIM
A_DECAY_RANK = 64
A_ICLR_RANK = 64
A_GATE_RANK = 128
A_COLS = 3 * A_WIDTH + A_DECAY_RANK + A_ICLR_RANK + A_GATE_RANK
A_GN_EPS = 64e-5
B_HEADS = 4
B_KEY_DIM = 256
B_VAL_DIM = 512
B_QK_WIDTH = B_HEADS * B_KEY_DIM
B_V_WIDTH = B_HEADS * B_VAL_DIM
B_COLS = 2 * B_QK_WIDTH + 2 * B_V_WIDTH
GATE_COLS = 2 * D_MODEL
FFN_HIDDEN = 4 * D_MODEL
ROPE_BASE = 10000.0
NORM_EPS = 1e-6
PAST_LEN = 4096

LANES = 128
A_PAIRS = A_WIDTH // LANES
A_CHUNK = 64
B_CHUNK = 128
ROW_TILE = 512
FFN_SLAB = 1024
VMEM_LIMIT = 56 * 1024 * 1024


def _dot(a, b):
    return jnp.dot(a.astype(BF16), b.astype(BF16), preferred_element_type=F32)


def _dot_nt(a, b):
    return lax.dot_general(a.astype(BF16), b.astype(BF16), (((1,), (1,)), ((), ())),
                           preferred_element_type=F32)


def _rms(x, g):
    return x * lax.rsqrt(jnp.mean(x * x, axis=-1, keepdims=True) + NORM_EPS) * g


def _params(*sem):
    return pltpu.CompilerParams(dimension_semantics=sem, vmem_limit_bytes=VMEM_LIMIT)


def _ada_kernel(c_ref, w_ref, b_ref, o_ref):
    c = c_ref[...]
    o_ref[...] = _dot(c * jax.nn.sigmoid(c), w_ref[...]) + b_ref[...]


def _ada(c, w, b):
    n, d = c.shape
    cols = w.shape[1]
    tn = cols // 4
    return pl.pallas_call(
        _ada_kernel,
        grid=(cols // tn,),
        in_specs=[pl.BlockSpec((n, d), lambda j: (0, 0)),
                  pl.BlockSpec((d, tn), lambda j: (0, j)),
                  pl.BlockSpec((1, tn), lambda j: (0, j))],
        out_specs=pl.BlockSpec((n, tn), lambda j: (0, j)),
        out_shape=jax.ShapeDtypeStruct((n, cols), F32),
        compiler_params=_params("arbitrary"),
        name="ada",
    )(c, w, b.reshape(1, cols))


def _prenorm_kernel(x_ref, g_ref, sc_ref, sh_ref, o_ref):
    h = _rms(x_ref[...], g_ref[...]) * (1.0 + sc_ref[0]) + sh_ref[0]
    o_ref[...] = h.astype(o_ref.dtype)


def _mod_spec(mod, tiles_per_group):
    _, r, d = mod.shape
    return pl.BlockSpec((1, r, d), lambda i: (i // tiles_per_group, 0, 0))


def _prenorm(x, g, sc, sh, tiles_per_group):
    n, d = x.shape
    return pl.pallas_call(
        _prenorm_kernel,
        grid=(n // ROW_TILE,),
        in_specs=[pl.BlockSpec((ROW_TILE, d), lambda i: (i, 0)),
                  pl.BlockSpec((1, d), lambda i: (0, 0)),
                  _mod_spec(sc, tiles_per_group), _mod_spec(sh, tiles_per_group)],
        out_specs=pl.BlockSpec((ROW_TILE, d), lambda i: (i, 0)),
        out_shape=jax.ShapeDtypeStruct((n, d), BF16),
        compiler_params=_params("arbitrary"),
        name="prenorm",
    )(x, g, sc, sh)


def _matmul_kernel(a_ref, w_ref, o_ref):
    o_ref[...] = jnp.dot(a_ref[...], w_ref[...], preferred_element_type=F32).astype(o_ref.dtype)


def _matmul(a, w, tn, name):
    n, k = a.shape
    cols = w.shape[1]
    tm = min(n, 1024)
    return pl.pallas_call(
        _matmul_kernel,
        grid=(cols // tn, n // tm),
        in_specs=[pl.BlockSpec((tm, k), lambda j, i: (i, 0)),
                  pl.BlockSpec((k, tn), lambda j, i: (0, j))],
        out_specs=pl.BlockSpec((tm, tn), lambda j, i: (i, j)),
        out_shape=jax.ShapeDtypeStruct((n, cols), BF16),
        compiler_params=_params("arbitrary", "arbitrary"),
        name=name,
    )(a, w)


def _split_hi_lo(x):
    hi = x.astype(BF16)
    lo = (x - hi.astype(F32)).astype(BF16)
    return hi, lo


def _head_sum(x, ones_bd):
    c = x.shape[0]
    xs = jnp.concatenate([x[:, p * LANES:(p + 1) * LANES] for p in range(A_PAIRS)], axis=0)
    hi, lo = _split_hi_lo(xs)
    s = (jnp.dot(hi, ones_bd, preferred_element_type=F32)
         + jnp.dot(lo, ones_bd, preferred_element_type=F32))
    return jnp.concatenate([s[p * c:(p + 1) * c] for p in range(A_PAIRS)], axis=1)


def _rwkv_kernel(za_ref, shift0_ref, sp0_ref, mu_ref, w0_ref, w2_ref, a0_ref, a2_ref, g2_ref,
                 kkp_ref, kap_ref, rk_ref, lnw_ref, lnb_ref,
                 ya_ref, sp_ref, shift_out_ref, carry_ref, *, valid_rows):
    C = A_CHUNK
    C2 = 2 * C
    c = pl.program_id(1)

    @pl.when(c == 0)
    def _():
        carry_ref[...] = shift0_ref[0]
        sp_ref[...] = sp0_ref[...]

    za = za_ref[...].astype(F32)
    row = lax.broadcasted_iota(jnp.int32, (C, 1), 0)
    za_prev = jnp.where(row == 0, carry_ref[...], pltpu.roll(za, 1, axis=0))
    carry_ref[...] = za[C - 1:C, :]
    shift_out_ref[0] = za[valid_rows - 1:valid_rows, :]
    mix = za + (za_prev - za) * mu_ref[...]

    r = mix[:, 0:A_WIDTH]
    k = mix[:, A_WIDTH:2 * A_WIDTH]
    v = mix[:, 2 * A_WIDTH:3 * A_WIDTH]
    zwi = mix[:, 3 * A_WIDTH:3 * A_WIDTH + LANES]
    zg = mix[:, 3 * A_WIDTH + LANES:A_COLS]

    lw = -math.exp(-0.5) * jax.nn.sigmoid(w0_ref[...] + _dot(jnp.tanh(zwi), w2_ref[...]))
    iclr = jax.nn.sigmoid(a0_ref[...] + _dot(zwi, a2_ref[...]))
    gate = _dot(jax.nn.sigmoid(zg), g2_ref[...])

    li = lax.broadcasted_iota(jnp.int32, (LANES, LANES), 0)
    lj = lax.broadcasted_iota(jnp.int32, (LANES, LANES), 1)
    blk_diag = (li // A_HEAD_DIM) == (lj // A_HEAD_DIM)
    ones_bd = jnp.where(blk_diag, 1.0, 0.0).astype(BF16)

    kk = k * kkp_ref[...]
    kk = kk * lax.rsqrt(jnp.maximum(_head_sum(kk * kk, ones_bd), 1e-24))
    k2 = k * (1.0 + (iclr - 1.0) * kap_ref[...])
    if valid_rows < C:
        valid = row < valid_rows
        lw = jnp.where(valid, lw, 0.0)
        kk = jnp.where(valid, kk, 0.0)
        k2 = jnp.where(valid, k2, 0.0)
    b = kk * iclr

    ti = lax.broadcasted_iota(jnp.int32, (C, C), 0)
    tj = lax.broadcasted_iota(jnp.int32, (C, C), 1)
    tril_incl = jnp.where(ti >= tj, 1.0, 0.0).astype(F32)
    cum = jnp.dot(tril_incl, lw, preferred_element_type=F32, precision=lax.Precision.HIGHEST)
    e_p = jnp.exp(cum)
    e_m = jnp.exp(-cum)
    e_tot = e_p[C - 1:C, :]
    e_rem = e_tot * e_m
    rt = r * e_p
    at = -kk * jnp.exp(cum - lw)
    bt = b * e_m
    kt = k2 * e_m
    kh = k2 * e_rem
    bh = b * e_rem

    qi = lax.broadcasted_iota(jnp.int32, (C2, C2), 0)
    qj = lax.broadcasted_iota(jnp.int32, (C2, C2), 1)
    fi, fj = qi % C, qj % C
    top = qi < C
    left = qj < C
    strict = fi > fj
    incl = fi >= fj
    mask0 = (top & strict) | (~top & incl)
    mask1 = (top & incl) | (~top & strict)
    blk_tl = top & left
    blk_br = (~top) & (~left)
    eye2 = jnp.where(qi == qj, 1.0, 0.0).astype(F32)
    lane = lax.broadcasted_iota(jnp.int32, (1, LANES), 1)
    m0 = lane < A_HEAD_DIM
    zeros = jnp.zeros((C, LANES), F32)
    n_double = int(math.log2(C)) - 1

    ys = []
    for p in range(A_PAIRS):
        sl = slice(p * LANES, (p + 1) * LANES)
        at_p, rt_p, bt_p, kt_p, v_p = at[:, sl], rt[:, sl], bt[:, sl], kt[:, sl], v[:, sl]
        l0 = jnp.concatenate([jnp.where(m0, at_p, 0.0), jnp.where(m0, rt_p, 0.0)], axis=0)
        a0 = jnp.where(mask0, _dot_nt(l0, jnp.concatenate([bt_p, kt_p], axis=0)), 0.0)
        l1 = jnp.concatenate([jnp.where(m0, 0.0, rt_p), jnp.where(m0, 0.0, at_p)], axis=0)
        a1 = jnp.where(mask1, _dot_nt(l1, jnp.concatenate([kt_p, bt_p], axis=0)), 0.0)

        nn = jnp.where(blk_tl, a0, 0.0) + jnp.where(blk_br, a1, 0.0)
        tt = eye2 + nn
        pw = nn
        for _ in range(n_double):
            pw = _dot(pw, pw)
            tt = tt + _dot(tt, pw)
        t01 = tt[0:C] + tt[C:C2]

        sp = sp_ref[0, p]
        v0 = jnp.where(m0, v_p, 0.0)
        v1 = jnp.where(m0, 0.0, v_p)
        w = (_dot_nt(at_p, sp)
             + _dot(a0[0:C], jnp.concatenate([zeros, v0], axis=0))
             + _dot(a1[C:C2], jnp.concatenate([v1, zeros], axis=0)))
        u = _dot(t01, jnp.concatenate([jnp.where(m0, w, 0.0), jnp.where(m0, 0.0, w)], axis=0))
        u0 = jnp.where(m0, u, 0.0)
        u1 = jnp.where(m0, 0.0, u)
        y = (_dot_nt(rt_p, sp)
             + _dot(a0[C:C2], jnp.concatenate([u0, v0], axis=0))
             + _dot(a1[0:C], jnp.concatenate([v1, u1], axis=0)))
        ys.append(y)

        vu = jnp.concatenate([v_p, u], axis=0)
        kb = jnp.concatenate([kh[:, sl], bh[:, sl]], axis=0)
        upd = _dot(vu.T, kb)
        sp_ref[0, p] = jnp.where(blk_diag, sp * e_tot[:, sl] + upd, 0.0)

    y = jnp.concatenate(ys, axis=1)
    mu = _head_sum(y, ones_bd) * (1.0 / A_HEAD_DIM)
    d = y - mu
    var = _head_sum(d * d, ones_bd) * (1.0 / A_HEAD_DIM)
    yn = d * lax.rsqrt(var + A_GN_EPS) * lnw_ref[...] + lnb_ref[...]
    bonus = _head_sum(r * k2 * rk_ref[...], ones_bd) * v
    ya_ref[...] = ((yn + bonus) * gate).astype(ya_ref.dtype)


def _rwkv(za, shift0, sp0, pr, n_chunks, valid_rows):
    bsz = shift0.shape[0]
    C = A_CHUNK
    vec = lambda n: pl.BlockSpec((1, n), lambda b, c: (0, 0))
    mat = lambda r, n: pl.BlockSpec((r, n), lambda b, c: (0, 0))
    return pl.pallas_call(
        functools.partial(_rwkv_kernel, valid_rows=valid_rows),
        grid=(bsz, n_chunks),
        in_specs=[pl.BlockSpec((C, A_COLS), lambda b, c: (b * n_chunks + c, 0)),
                  pl.BlockSpec((1, 1, A_COLS), lambda b, c: (b, 0, 0)),
                  pl.BlockSpec((1, A_PAIRS, LANES, LANES), lambda b, c: (b, 0, 0, 0)),
                  vec(A_COLS), vec(A_WIDTH), mat(LANES, A_WIDTH), vec(A_WIDTH), mat(LANES, A_WIDTH),
                  mat(A_GATE_RANK, A_WIDTH), vec(A_WIDTH), vec(A_WIDTH), vec(A_WIDTH),
                  vec(A_WIDTH), vec(A_WIDTH)],
        out_specs=[pl.BlockSpec((C, A_WIDTH), lambda b, c: (b * n_chunks + c, 0)),
                   pl.BlockSpec((1, A_PAIRS, LANES, LANES), lambda b, c: (b, 0, 0, 0)),
                   pl.BlockSpec((1, 1, A_COLS), lambda b, c: (b, 0, 0))],
        out_shape=[jax.ShapeDtypeStruct((bsz * n_chunks * C, A_WIDTH), BF16),
                   jax.ShapeDtypeStruct((bsz, A_PAIRS, LANES, LANES), F32),
                   jax.ShapeDtypeStruct((bsz, 1, A_COLS), F32)],
        scratch_shapes=[pltpu.VMEM((1, A_COLS), F32)],
        compiler_params=_params("arbitrary", "arbitrary"),
        name="rwkv",
    )(za, shift0, sp0, pr["a_mu"], pr["a_w0"], pr["a_w2p"], pr["a_a0"], pr["a_a2p"], pr["a_g2"],
      pr["a_kk"], pr["a_ka"], pr["a_rk"], pr["a_ln_w"], pr["a_ln_b"])


def _pack_pairs(s):
    bsz = s.shape[0]
    s = s.reshape(bsz, A_PAIRS, 2, A_HEAD_DIM, A_HEAD_DIM)
    z = jnp.zeros_like(s[:, :, 0])
    top = jnp.concatenate([s[:, :, 0], z], axis=-1)
    bot = jnp.concatenate([z, s[:, :, 1]], axis=-1)
    return jnp.concatenate([top, bot], axis=-2)


def _unpack_pairs(sp):
    h = A_HEAD_DIM
    s = jnp.stack([sp[:, :, :h, :h], sp[:, :, h:, h:]], axis=2)
    return s.reshape(sp.shape[0], A_HEADS, h, h)


def _ret_kernel(zb_ref, cos_ref, sin_ref, s0_ref, yb_ref, s_ref, *, valid_rows):
    C = B_CHUNK
    c = pl.program_id(1)

    @pl.when(c == 0)
    def _():
        s_ref[...] = s0_ref[...]

    cos = cos_ref[...]
    sin = sin_ref[...]
    half = B_KEY_DIM // 2
    ri = lax.broadcasted_iota(jnp.int32, (C, C), 0)
    rj = lax.broadcasted_iota(jnp.int32, (C, C), 1)
    diff = (ri - rj).astype(F32)
    idx = lax.broadcasted_iota(jnp.int32, (C, 1), 0).astype(F32)

    def rope(x):
        x1, x2 = x[:, :half], x[:, half:]
        return jnp.concatenate([x1 * cos - x2 * sin, x1 * sin + x2 * cos], axis=1)

    for h in range(B_HEADS):
        log_g = math.log(1.0 - 2.0 ** (-5.0 - h))
        q = rope(zb_ref[:, h * B_KEY_DIM:(h + 1) * B_KEY_DIM].astype(F32))
        k = rope(zb_ref[:, B_QK_WIDTH + h * B_KEY_DIM:B_QK_WIDTH + (h + 1) * B_KEY_DIM].astype(F32))
        k = k * (B_KEY_DIM ** -0.5)
        v = zb_ref[:, 2 * B_QK_WIDTH + h * B_VAL_DIM:2 * B_QK_WIDTH + (h + 1) * B_VAL_DIM]
        g = zb_ref[:, 2 * B_QK_WIDTH + B_V_WIDTH + h * B_VAL_DIM:
                   2 * B_QK_WIDTH + B_V_WIDTH + (h + 1) * B_VAL_DIM].astype(F32)
        mask = jnp.where(diff >= 0.0, jnp.exp(jnp.maximum(diff, 0.0) * log_g), 0.0)
        scores = _dot_nt(q, k) * mask
        s = s_ref[0, h]
        y = _dot(scores, v) + _dot(q * jnp.exp((idx + 1.0) * log_g), s)
        k_dec = jnp.where(idx < valid_rows, k * jnp.exp((valid_rows - 1.0 - idx) * log_g), 0.0)
        s_ref[0, h] = s * math.exp(valid_rows * log_g) + _dot(k_dec.T, v)
        yn = y * lax.rsqrt(jnp.mean(y * y, axis=-1, keepdims=True) + NORM_EPS)
        yb_ref[:, h * B_VAL_DIM:(h + 1) * B_VAL_DIM] = (g * jax.nn.sigmoid(g) * yn).astype(yb_ref.dtype)


def _ret(zb, cos, sin, s0, n_chunks, valid_rows):
    bsz = s0.shape[0]
    C = B_CHUNK
    half = B_KEY_DIM // 2
    state_spec = pl.BlockSpec((1, B_HEADS, B_KEY_DIM, B_VAL_DIM), lambda b, c: (b, 0, 0, 0))
    return pl.pallas_call(
        functools.partial(_ret_kernel, valid_rows=valid_rows),
        grid=(bsz, n_chunks),
        in_specs=[pl.BlockSpec((C, B_COLS), lambda b, c: (b * n_chunks + c, 0)),
                  pl.BlockSpec((C, half), lambda b, c: (c, 0)),
                  pl.BlockSpec((C, half), lambda b, c: (c, 0)),
                  state_spec],
        out_specs=[pl.BlockSpec((C, B_V_WIDTH), lambda b, c: (b * n_chunks + c, 0)), state_spec],
        out_shape=[jax.ShapeDtypeStruct((bsz * n_chunks * C, B_V_WIDTH), BF16),
                   jax.ShapeDtypeStruct(s0.shape, F32)],
        compiler_params=_params("arbitrary", "arbitrary"),
        name="ret",
    )(zb, cos, sin, s0)


def _merge_kernel(ya_ref, yb_ref, zg_ref, x_ref, gt_ref, g_ref, wa_ref, wb_ref, wo_ref, o_ref):
    pa = jnp.dot(ya_ref[...], wa_ref[...], preferred_element_type=F32)
    pb = jnp.dot(yb_ref[...], wb_ref[...], preferred_element_type=F32)
    ga = jax.nn.sigmoid(zg_ref[:, :D_MODEL].astype(F32))
    gb = jax.nn.sigmoid(zg_ref[:, D_MODEL:].astype(F32))
    out = _dot(ga * pa + gb * pb, wo_ref[...])
    o_ref[...] = x_ref[...] + gt_ref[0] * _rms(out, g_ref[...])


def _resident(shape):
    return pl.BlockSpec(shape, lambda i: (0,) * len(shape), pipeline_mode=pl.Buffered(1))


def _merge(ya, yb, zg, x, gt, g, wa, wb, wo, tiles_per_group):
    n, d = x.shape
    rows = lambda w: pl.BlockSpec((ROW_TILE, w), lambda i: (i, 0))
    return pl.pallas_call(
        _merge_kernel,
        grid=(n // ROW_TILE,),
        in_specs=[rows(A_WIDTH), rows(B_V_WIDTH), rows(GATE_COLS), rows(d),
                  _mod_spec(gt, tiles_per_group), _resident((1, d)),
                  _resident(wa.shape), _resident(wb.shape), _resident(wo.shape)],
        out_specs=rows(d),
        out_shape=jax.ShapeDtypeStruct((n, d), F32),
        compiler_params=_params("arbitrary"),
        name="merge",
    )(ya, yb, zg, x, gt, g, wa, wb, wo)


def _ffn_kernel(x_ref, sc_ref, sh_ref, gt_ref, gpre_ref, gpost_ref, up_ref, down_ref, o_ref):
    x = x_ref[...]
    h = (_rms(x, gpre_ref[...]) * (1.0 + sc_ref[0]) + sh_ref[0]).astype(BF16)
    f = jnp.zeros(x.shape, F32)
    for j in range(FFN_HIDDEN // FFN_SLAB):
        sl = slice(j * FFN_SLAB, (j + 1) * FFN_SLAB)
        a = jnp.maximum(jnp.dot(h, up_ref[:, sl], preferred_element_type=F32), 0.0)
        f = f + _dot(a * a, down_ref[sl, :])
    o_ref[...] = x + gt_ref[0] * _rms(f, gpost_ref[...])


def _ffn(x, sc, sh, gt, gpre, gpost, up, down, tiles_per_group):
    n, d = x.shape
    rows = pl.BlockSpec((ROW_TILE, d), lambda i: (i, 0))
    mod = _mod_spec(sc, tiles_per_group)
    return pl.pallas_call(
        _ffn_kernel,
        grid=(n // ROW_TILE,),
        in_specs=[rows, mod, mod, mod, _resident((1, d)), _resident((1, d)),
                  _resident(up.shape), _resident(down.shape)],
        out_specs=rows,
        out_shape=jax.ShapeDtypeStruct((n, d), F32),
        compiler_params=_params("arbitrary"),
        name="ffn",
    )(x, sc, sh, gt, gpre, gpost, up, down)


def _pad_frames(z, bsz, t, t_pad):
    if t_pad == t:
        return z
    z = z.reshape(bsz, t, -1)
    return jnp.pad(z, ((0, 0), (0, t_pad - t), (0, 0))).reshape(bsz * t_pad, -1)


def _drop_frames(y, bsz, t, t_pad):
    if t_pad == t:
        return y
    return y.reshape(bsz, t_pad, -1)[:, :t].reshape(bsz * t, -1)


def _layer(x, mod, shift0, sp0, ret0, pos0, pr, bsz, t):
    n = bsz * t
    if t % ROW_TILE == 0:
        tiles_per_group = t // ROW_TILE
        mods = [m.reshape(bsz, 1, D_MODEL) for m in jnp.split(mod, 6, axis=-1)]
    else:
        tiles_per_group = 1
        mods = [jnp.repeat(m, t, axis=0).reshape(n // ROW_TILE, ROW_TILE, D_MODEL)
                for m in jnp.split(mod, 6, axis=-1)]
    sh1, sc1, gt1, sh2, sc2, gt2 = mods

    h = _prenorm(x, pr["norm_pre0"], sc1, sh1, tiles_per_group)
    za = _matmul(h, pr["w_in_a"], A_COLS // 2, "in_proj_a")
    zb = _matmul(h, pr["w_in_b"], B_COLS // 4, "in_proj_b")
    zg = _matmul(h, pr["w_in_g"], GATE_COLS, "in_proj_g")

    ca = min(t, A_CHUNK)
    ta = -(-t // A_CHUNK) * A_CHUNK
    ya, sp, shift = _rwkv(_pad_frames(za, bsz, t, ta), shift0, sp0, pr, ta // A_CHUNK, ca)
    ya = _drop_frames(ya, bsz, t, ta)

    cb = min(t, B_CHUNK)
    tb = -(-t // B_CHUNK) * B_CHUNK
    pos = (pos0 + jnp.arange(tb, dtype=jnp.int32)).astype(F32)
    half = B_KEY_DIM // 2
    inv_freq = ROPE_BASE ** (-jnp.linspace(0.0, 1.0, half, dtype=F32))
    ang = pos[:, None] * inv_freq[None, :]
    yb, ret = _ret(_pad_frames(zb, bsz, t, tb), jnp.cos(ang), jnp.sin(ang), ret0, tb // B_CHUNK, cb)
    yb = _drop_frames(yb, bsz, t, tb)

    x = _merge(ya, yb, zg, x, gt1, pr["norm_post0"], pr["w_branch_a"], pr["w_branch_b"], pr["w_out"],
               tiles_per_group)
    x = _ffn(x, sc2, sh2, gt2, pr["norm_pre1"], pr["norm_post1"], pr["w_ffn_up"], pr["w_ffn_down"],
             tiles_per_group)
    return x, shift.reshape(bsz, A_COLS), _unpack_pairs(sp), ret


def kernel(x_prompt, x_sample, state_shift, state_rwkv, state_ret, c_prompt, c_sample, ada_w, ada_b, norm_pre, norm_post, w_in, a_mu, a_w0, a_w2, a_a0, a_a2, a_g2, a_kk, a_ka, a_rk, a_ln_w, a_ln_b, w_branch_a, w_branch_b, w_out, w_ffn_up, w_ffn_down):
    depth = w_in.shape[0]
    bp, tp, _ = x_prompt.shape
    bs, ts, _ = x_sample.shape
    xp = x_prompt.reshape(bp * tp, D_MODEL)
    xs = x_sample.reshape(bs * ts, D_MODEL)
    c_all = jnp.concatenate([c_prompt, c_sample], axis=0)
    row = lambda a: a.reshape(1, -1)
    lora_pad = jnp.zeros((A_DECAY_RANK, A_WIDTH), BF16)

    zero_shift = jnp.zeros((bp, 1, A_COLS), F32)
    zero_sp = jnp.zeros((bp, A_PAIRS, LANES, LANES), F32)
    zero_ret = jnp.zeros((bp, B_HEADS, B_KEY_DIM, B_VAL_DIM), F32)

    outs = [[] for _ in range(6)]
    for l in range(depth):
        wl = w_in[l].astype(BF16)
        pr = {
            "norm_pre0": row(norm_pre[l, 0]), "norm_pre1": row(norm_pre[l, 1]),
            "norm_post0": row(norm_post[l, 0]), "norm_post1": row(norm_post[l, 1]),
            "w_in_a": wl[:, :A_COLS], "w_in_b": wl[:, A_COLS:A_COLS + B_COLS],
            "w_in_g": wl[:, A_COLS + B_COLS:],
            "a_mu": row(a_mu[l]), "a_w0": row(a_w0[l]), "a_a0": row(a_a0[l]),
            "a_w2p": jnp.concatenate([a_w2[l].astype(BF16), lora_pad], axis=0),
            "a_a2p": jnp.concatenate([lora_pad, a_a2[l].astype(BF16)], axis=0),
            "a_g2": a_g2[l].astype(BF16),
            "a_kk": row(a_kk[l]), "a_ka": row(a_ka[l]), "a_rk": row(a_rk[l]),
            "a_ln_w": row(a_ln_w[l]), "a_ln_b": row(a_ln_b[l]),
            "w_branch_a": w_branch_a[l].astype(BF16), "w_branch_b": w_branch_b[l].astype(BF16),
            "w_out": w_out[l].astype(BF16),
            "w_ffn_up": w_ffn_up[l].astype(BF16), "w_ffn_down": w_ffn_down[l].astype(BF16),
        }
        mod = _ada(c_all, ada_w[l].astype(BF16), ada_b[l])
        xp, sp_, rp_, tp_ = _layer(xp, mod[:bp], zero_shift, zero_sp, zero_ret, 0, pr, bp, tp)
        xs, ss_, rs_, ts_ = _layer(xs, mod[bp:], state_shift[l].reshape(bs, 1, A_COLS),
                                   _pack_pairs(state_rwkv[l]), state_ret[l], PAST_LEN, pr, bs, ts)
        for lst, val in zip(outs, (sp_, rp_, tp_, ss_, rs_, ts_)):
            lst.append(val)

    stacked = [jnp.stack(o) for o in outs]
    return (xp.reshape(bp, tp, D_MODEL), xs.reshape(bs, ts, D_MODEL), *stacked)
```

```python
import functools
import math

import jax
import jax.numpy as jnp
from jax import lax
from jax.experimental import pallas as pl
from jax.experimental.pallas import tpu as pltpu

F32 = jnp.float32
BF16 = jnp.bfloat16

D_MODEL = 1024
A_HEADS = 16
A_HEAD_DIM = 64
A_WIDTH = A_HEADS * A_HEAD_DIM
A_DECAY_RANK = 64
A_ICLR_RANK = 64
A_GATE_RANK = 128
A_COLS = 3 * A_WIDTH + A_DECAY_RANK + A_ICLR_RANK + A_GATE_RANK
A_GN_EPS = 64e-5
B_HEADS = 4
B_KEY_DIM = 256
B_VAL_DIM = 512
B_QK_WIDTH = B_HEADS * B_KEY_DIM
B_V_WIDTH = B_HEADS * B_VAL_DIM
B_COLS = 2 * B_QK_WIDTH + 2 * B_V_WIDTH
GATE_COLS = 2 * D_MODEL
FFN_HIDDEN = 4 * D_MODEL
ROPE_BASE = 10000.0
NORM_EPS = 1e-6
PAST_LEN = 4096

LANES = 128
MXU_DIM = 256
A_PAIRS = A_WIDTH // LANES
A_CHUNK = 64
B_CHUNK = 128
ROW_TILE = 512
FFN_SLAB = 1024
VMEM_LIMIT = 56 * 1024 * 1024


def _dot(a, b):
    return jnp.dot(a.astype(BF16), b.astype(BF16), preferred_element_type=F32)


def _dot_nt(a, b):
    return lax.dot_general(a.astype(BF16), b.astype(BF16), (((1,), (1,)), ((), ())),
                           preferred_element_type=F32)


def _rms(x, g):
    return x * lax.rsqrt(jnp.mean(x * x, axis=-1, keepdims=True) + NORM_EPS) * g


def _params(*sem):
    return pltpu.CompilerParams(dimension_semantics=sem, vmem_limit_bytes=VMEM_LIMIT)


def _ada_kernel(c_ref, w_ref, b_ref, o_ref):
    c = c_ref[...]
    o_ref[...] = _dot(c * jax.nn.sigmoid(c), w_ref[...]) + b_ref[...]


def _ada(c, w, b):
    n, d = c.shape
    cols = w.shape[1]
    tn = cols // 4
    return pl.pallas_call(
        _ada_kernel,
        grid=(cols // tn,),
        in_specs=[pl.BlockSpec((n, d), lambda j: (0, 0)),
                  pl.BlockSpec((d, tn), lambda j: (0, j)),
                  pl.BlockSpec((1, tn), lambda j: (0, j))],
        out_specs=pl.BlockSpec((n, tn), lambda j: (0, j)),
        out_shape=jax.ShapeDtypeStruct((n, cols), F32),
        compiler_params=_params("arbitrary"),
        name="ada",
    )(c, w, b.reshape(1, cols))


def _prenorm_kernel(x_ref, g_ref, sc_ref, sh_ref, o_ref):
    h = _rms(x_ref[...], g_ref[...]) * (1.0 + sc_ref[0]) + sh_ref[0]
    o_ref[...] = h.astype(o_ref.dtype)


def _mod_spec(mod, tiles_per_group):
    _, r, d = mod.shape
    return pl.BlockSpec((1, r, d), lambda i: (i // tiles_per_group, 0, 0))


def _prenorm(x, g, sc, sh, tiles_per_group):
    n, d = x.shape
    return pl.pallas_call(
        _prenorm_kernel,
        grid=(n // ROW_TILE,),
        in_specs=[pl.BlockSpec((ROW_TILE, d), lambda i: (i, 0)),
                  pl.BlockSpec((1, d), lambda i: (0, 0)),
                  _mod_spec(sc, tiles_per_group), _mod_spec(sh, tiles_per_group)],
        out_specs=pl.BlockSpec((ROW_TILE, d), lambda i: (i, 0)),
        out_shape=jax.ShapeDtypeStruct((n, d), BF16),
        compiler_params=_params("arbitrary"),
        name="prenorm",
    )(x, g, sc, sh)


def _matmul_kernel(a_ref, w_ref, o_ref):
    o_ref[...] = jnp.dot(a_ref[...], w_ref[...], preferred_element_type=F32).astype(o_ref.dtype)


def _matmul(a, w, tn, name):
    n, k = a.shape
    cols = w.shape[1]
    tm = min(n, 1024)
    return pl.pallas_call(
        _matmul_kernel,
        grid=(cols // tn, n // tm),
        in_specs=[pl.BlockSpec((tm, k), lambda j, i: (i, 0)),
                  pl.BlockSpec((k, tn), lambda j, i: (0, j))],
        out_specs=pl.BlockSpec((tm, tn), lambda j, i: (i, j)),
        out_shape=jax.ShapeDtypeStruct((n, cols), BF16),
        compiler_params=_params("arbitrary", "arbitrary"),
        name=name,
    )(a, w)


def _split3(x):
    h1 = x.astype(BF16)
    r1 = x - h1.astype(F32)
    h2 = r1.astype(BF16)
    h3 = (r1 - h2.astype(F32)).astype(BF16)
    return h1, h2, h3


def _head_sum(x, ones_bd):
    c = x.shape[0]
    groups = A_WIDTH // MXU_DIM
    xs = jnp.concatenate([x[:, q * MXU_DIM:(q + 1) * MXU_DIM] for q in range(groups)], axis=0)
    s = jnp.dot(xs.astype(BF16), ones_bd, preferred_element_type=F32)
    return jnp.concatenate([s[q * c:(q + 1) * c] for q in range(groups)], axis=1)


def _rwkv_kernel(za_ref, shift0_ref, sp0_ref, mu_ref, w0_ref, w2_ref, a0_ref, a2_ref, g2_ref,
                 kkp_ref, kap_ref, rk_ref, lnw_ref, lnb_ref,
                 ya_ref, sp_ref, shift_out_ref, carry_ref, *, valid_rows):
    C = A_CHUNK
    C2 = 2 * C
    c = pl.program_id(1)

    @pl.when(c == 0)
    def _():
        carry_ref[...] = shift0_ref[0]
        sp_ref[...] = sp0_ref[...]

    za = za_ref[...].astype(F32)
    row = lax.broadcasted_iota(jnp.int32, (C, 1), 0)
    za_prev = jnp.where(row == 0, carry_ref[...], pltpu.roll(za, 1, axis=0))
    carry_ref[...] = za[C - 1:C, :]
    shift_out_ref[0] = za[valid_rows - 1:valid_rows, :]
    mix = za + (za_prev - za) * mu_ref[...]

    r = mix[:, 0:A_WIDTH]
    k = mix[:, A_WIDTH:2 * A_WIDTH]
    v = mix[:, 2 * A_WIDTH:3 * A_WIDTH]
    zwi = mix[:, 3 * A_WIDTH:3 * A_WIDTH + LANES]
    zg = mix[:, 3 * A_WIDTH + LANES:A_COLS]

    lw = -math.exp(-0.5) * jax.nn.sigmoid(w0_ref[...] + _dot(jnp.tanh(zwi), w2_ref[...]))
    iclr = jax.nn.sigmoid(a0_ref[...] + _dot(zwi, a2_ref[...]))
    gate = _dot(jax.nn.sigmoid(zg), g2_ref[...])

    li = lax.broadcasted_iota(jnp.int32, (LANES, LANES), 0)
    lj = lax.broadcasted_iota(jnp.int32, (LANES, LANES), 1)
    blk_diag = (li // A_HEAD_DIM) == (lj // A_HEAD_DIM)
    mi = lax.broadcasted_iota(jnp.int32, (MXU_DIM, MXU_DIM), 0)
    mj = lax.broadcasted_iota(jnp.int32, (MXU_DIM, MXU_DIM), 1)
    ones_bd = jnp.where((mi // A_HEAD_DIM) == (mj // A_HEAD_DIM), 1.0, 0.0).astype(BF16)

    kk = k * kkp_ref[...]
    kk = kk * lax.rsqrt(jnp.maximum(_head_sum(kk * kk, ones_bd), 1e-24))
    k2 = k * (1.0 + (iclr - 1.0) * kap_ref[...])
    if valid_rows < C:
        valid = row < valid_rows
        lw = jnp.where(valid, lw, 0.0)
        kk = jnp.where(valid, kk, 0.0)
        k2 = jnp.where(valid, k2, 0.0)
    b = kk * iclr

    ti = lax.broadcasted_iota(jnp.int32, (C, C), 0)
    tj = lax.broadcasted_iota(jnp.int32, (C, C), 1)
    tril_incl = jnp.where(ti >= tj, 1.0, 0.0).astype(BF16)
    cum = sum(jnp.dot(tril_incl, part, preferred_element_type=F32) for part in _split3(lw))
    e_p = jnp.exp(cum)
    e_m = jnp.exp(-cum)
    e_tot = e_p[C - 1:C, :]
    e_rem = e_tot * e_m
    rt = r * e_p
    at = -kk * jnp.exp(cum - lw)
    bt = b * e_m
    kt = k2 * e_m
    kh = k2 * e_rem
    bh = b * e_rem

    qi = lax.broadcasted_iota(jnp.int32, (C2, C2), 0)
    qj = lax.broadcasted_iota(jnp.int32, (C2, C2), 1)
    fi, fj = qi % C, qj % C
    top = qi < C
    left = qj < C
    strict = fi > fj
    incl = fi >= fj
    mask0 = (top & strict) | (~top & incl)
    mask1 = (top & incl) | (~top & strict)
    blk_tl = top & left
    blk_br = (~top) & (~left)
    eye2 = jnp.where(qi == qj, 1.0, 0.0).astype(F32)
    lane = lax.broadcasted_iota(jnp.int32, (1, LANES), 1)
    m0 = lane < A_HEAD_DIM
    zeros = jnp.zeros((C, LANES), F32)
    n_double = int(math.log2(C)) - 1

    pairs = range(A_PAIRS)
    sls = [slice(p * LANES, (p + 1) * LANES) for p in pairs]
    cat = lambda *xs: jnp.concatenate(xs, axis=0)
    only0 = lambda x: jnp.where(m0, x, 0.0)
    only1 = lambda x: jnp.where(m0, 0.0, x)
    a0s = [jnp.where(mask0, _dot_nt(cat(only0(at[:, s]), only0(rt[:, s])), cat(bt[:, s], kt[:, s])), 0.0)
           for s in sls]
    a1s = [jnp.where(mask1, _dot_nt(cat(only1(rt[:, s]), only1(at[:, s])), cat(kt[:, s], bt[:, s])), 0.0)
           for s in sls]

    pws = [jnp.where(blk_tl, a0, 0.0) + jnp.where(blk_br, a1, 0.0) for a0, a1 in zip(a0s, a1s)]
    tts = [eye2 + nn for nn in pws]
    pws = [_dot(pw, pw) for pw in pws]
    for _ in range(n_double - 1):
        both = [_dot(cat(pw, tt), pw) for pw, tt in zip(pws, tts)]
        pws = [bth[0:C2] for bth in both]
        tts = [tt + bth[C2:2 * C2] for tt, bth in zip(tts, both)]
    tts = [tt + _dot(tt, pw) for tt, pw in zip(tts, pws)]
    t01s = [tt[0:C] + tt[C:C2] for tt in tts]

    sps = [sp_ref[0, p] for p in pairs]
    v0s = [only0(v[:, s]) for s in sls]
    v1s = [only1(v[:, s]) for s in sls]
    ws = [_dot_nt(at[:, s], sp) + _dot(a0[0:C], cat(zeros, v0)) + _dot(a1[C:C2], cat(v1, zeros))
          for s, sp, a0, a1, v0, v1 in zip(sls, sps, a0s, a1s, v0s, v1s)]
    us = [_dot(t01, cat(only0(w), only1(w))) for t01, w in zip(t01s, ws)]
    ys = [_dot_nt(rt[:, s], sp) + _dot(a0[C:C2], cat(only0(u), v0)) + _dot(a1[0:C], cat(v1, only1(u)))
          for s, sp, a0, a1, v0, v1, u in zip(sls, sps, a0s, a1s, v0s, v1s, us)]
    for p, s, sp, u in zip(pairs, sls, sps, us):
        upd = _dot(cat(v[:, s], u).T, cat(kh[:, s], bh[:, s]))
        sp_ref[0, p] = jnp.where(blk_diag, sp * e_tot[:, s] + upd, 0.0)

    y = jnp.concatenate(ys, axis=1)
    mu = _head_sum(y, ones_bd) * (1.0 / A_HEAD_DIM)
    d = y - mu
    var = _head_sum(d * d, ones_bd) * (1.0 / A_HEAD_DIM)
    yn = d * lax.rsqrt(var + A_GN_EPS) * lnw_ref[...] + lnb_ref[...]
    bonus = _head_sum(r * k2 * rk_ref[...], ones_bd) * v
    ya_ref[...] = ((yn + bonus) * gate).astype(ya_ref.dtype)


def _rwkv(za, shift0, sp0, pr, n_chunks, valid_rows):
    bsz = shift0.shape[0]
    C = A_CHUNK
    vec = lambda n: pl.BlockSpec((1, n), lambda b, c: (0, 0))
    mat = lambda r, n: pl.BlockSpec((r, n), lambda b, c: (0, 0))
    return pl.pallas_call(
        functools.partial(_rwkv_kernel, valid_rows=valid_rows),
        grid=(bsz, n_chunks),
        in_specs=[pl.BlockSpec((C, A_COLS), lambda b, c: (b * n_chunks + c, 0)),
                  pl.BlockSpec((1, 1, A_COLS), lambda b, c: (b, 0, 0)),
                  pl.BlockSpec((1, A_PAIRS, LANES, LANES), lambda b, c: (b, 0, 0, 0)),
                  vec(A_COLS), vec(A_WIDTH), mat(LANES, A_WIDTH), vec(A_WIDTH), mat(LANES, A_WIDTH),
                  mat(A_GATE_RANK, A_WIDTH), vec(A_WIDTH), vec(A_WIDTH), vec(A_WIDTH),
                  vec(A_WIDTH), vec(A_WIDTH)],
        out_specs=[pl.BlockSpec((C, A_WIDTH), lambda b, c: (b * n_chunks + c, 0)),
                   pl.BlockSpec((1, A_PAIRS, LANES, LANES), lambda b, c: (b, 0, 0, 0)),
                   pl.BlockSpec((1, 1, A_COLS), lambda b, c: (b, 0, 0))],
        out_shape=[jax.ShapeDtypeStruct((bsz * n_chunks * C, A_WIDTH), BF16),
                   jax.ShapeDtypeStruct((bsz, A_PAIRS, LANES, LANES), F32),
                   jax.ShapeDtypeStruct((bsz, 1, A_COLS), F32)],
        scratch_shapes=[pltpu.VMEM((1, A_COLS), F32)],
        compiler_params=_params("arbitrary", "arbitrary"),
        name="rwkv",
    )(za, shift0, sp0, pr["a_mu"], pr["a_w0"], pr["a_w2p"], pr["a_a0"], pr["a_a2p"], pr["a_g2"],
      pr["a_kk"], pr["a_ka"], pr["a_rk"], pr["a_ln_w"], pr["a_ln_b"])


def _pack_pairs(s):
    bsz = s.shape[0]
    s = s.reshape(bsz, A_PAIRS, 2, A_HEAD_DIM, A_HEAD_DIM)
    z = jnp.zeros_like(s[:, :, 0])
    top = jnp.concatenate([s[:, :, 0], z], axis=-1)
    bot = jnp.concatenate([z, s[:, :, 1]], axis=-1)
    return jnp.concatenate([top, bot], axis=-2)


def _unpack_pairs(sp):
    h = A_HEAD_DIM
    s = jnp.stack([sp[:, :, :h, :h], sp[:, :, h:, h:]], axis=2)
    return s.reshape(sp.shape[0], A_HEADS, h, h)


def _ret_kernel(zb_ref, cos_ref, sin_ref, s0_ref, yb_ref, s_ref, *, valid_rows):
    C = B_CHUNK
    c = pl.program_id(1)

    @pl.when(c == 0)
    def _():
        s_ref[...] = s0_ref[...]

    cos = cos_ref[...]
    sin = sin_ref[...]
    half = B_KEY_DIM // 2
    ri = lax.broadcasted_iota(jnp.int32, (C, C), 0)
    rj = lax.broadcasted_iota(jnp.int32, (C, C), 1)
    diff = (ri - rj).astype(F32)
    idx = lax.broadcasted_iota(jnp.int32, (C, 1), 0).astype(F32)

    def rope(x):
        x1, x2 = x[:, :half], x[:, half:]
        return jnp.concatenate([x1 * cos - x2 * sin, x1 * sin + x2 * cos], axis=1)

    for h in range(B_HEADS):
        log_g = math.log(1.0 - 2.0 ** (-5.0 - h))
        q = rope(zb_ref[:, h * B_KEY_DIM:(h + 1) * B_KEY_DIM].astype(F32))
        k = rope(zb_ref[:, B_QK_WIDTH + h * B_KEY_DIM:B_QK_WIDTH + (h + 1) * B_KEY_DIM].astype(F32))
        k = k * (B_KEY_DIM ** -0.5)
        v = zb_ref[:, 2 * B_QK_WIDTH + h * B_VAL_DIM:2 * B_QK_WIDTH + (h + 1) * B_VAL_DIM]
        g = zb_ref[:, 2 * B_QK_WIDTH + B_V_WIDTH + h * B_VAL_DIM:
                   2 * B_QK_WIDTH + B_V_WIDTH + (h + 1) * B_VAL_DIM].astype(F32)
        mask = jnp.where(diff >= 0.0, jnp.exp(jnp.maximum(diff, 0.0) * log_g), 0.0)
        scores = _dot_nt(q, k) * mask
        s = s_ref[0, h]
        y = _dot(scores, v) + _dot(q * jnp.exp((idx + 1.0) * log_g), s)
        k_dec = jnp.where(idx < valid_rows, k * jnp.exp((valid_rows - 1.0 - idx) * log_g), 0.0)
        s_ref[0, h] = s * math.exp(valid_rows * log_g) + _dot(k_dec.T, v)
        yn = y * lax.rsqrt(jnp.mean(y * y, axis=-1, keepdims=True) + NORM_EPS)
        yb_ref[:, h * B_VAL_DIM:(h + 1) * B_VAL_DIM] = (g * jax.nn.sigmoid(g) * yn).astype(yb_ref.dtype)


def _ret(zb, cos, sin, s0, n_chunks, valid_rows):
    bsz = s0.shape[0]
    C = B_CHUNK
    half = B_KEY_DIM // 2
    state_spec = pl.BlockSpec((1, B_HEADS, B_KEY_DIM, B_VAL_DIM), lambda b, c: (b, 0, 0, 0))
    return pl.pallas_call(
        functools.partial(_ret_kernel, valid_rows=valid_rows),
        grid=(bsz, n_chunks),
        in_specs=[pl.BlockSpec((C, B_COLS), lambda b, c: (b * n_chunks + c, 0)),
                  pl.BlockSpec((C, half), lambda b, c: (c, 0)),
                  pl.BlockSpec((C, half), lambda b, c: (c, 0)),
                  state_spec],
        out_specs=[pl.BlockSpec((C, B_V_WIDTH), lambda b, c: (b * n_chunks + c, 0)), state_spec],
        out_shape=[jax.ShapeDtypeStruct((bsz * n_chunks * C, B_V_WIDTH), BF16),
                   jax.ShapeDtypeStruct(s0.shape, F32)],
        compiler_params=_params("arbitrary", "arbitrary"),
        name="ret",
    )(zb, cos, sin, s0)


def _merge_kernel(ya_ref, yb_ref, zg_ref, x_ref, gt_ref, g_ref, wa_ref, wb_ref, wo_ref, o_ref):
    pa = jnp.dot(ya_ref[...], wa_ref[...], preferred_element_type=F32)
    pb = jnp.dot(yb_ref[...], wb_ref[...], preferred_element_type=F32)
    ga = jax.nn.sigmoid(zg_ref[:, :D_MODEL].astype(F32))
    gb = jax.nn.sigmoid(zg_ref[:, D_MODEL:].astype(F32))
    out = _dot(ga * pa + gb * pb, wo_ref[...])
    o_ref[...] = x_ref[...] + gt_ref[0] * _rms(out, g_ref[...])


def _resident(shape):
    return pl.BlockSpec(shape, lambda i: (0,) * len(shape), pipeline_mode=pl.Buffered(1))


def _merge(ya, yb, zg, x, gt, g, wa, wb, wo, tiles_per_group):
    n, d = x.shape
    rows = lambda w: pl.BlockSpec((ROW_TILE, w), lambda i: (i, 0))
    return pl.pallas_call(
        _merge_kernel,
        grid=(n // ROW_TILE,),
        in_specs=[rows(A_WIDTH), rows(B_V_WIDTH), rows(GATE_COLS), rows(d),
                  _mod_spec(gt, tiles_per_group), _resident((1, d)),
                  _resident(wa.shape), _resident(wb.shape), _resident(wo.shape)],
        out_specs=rows(d),
        out_shape=jax.ShapeDtypeStruct((n, d), F32),
        compiler_params=_params("arbitrary"),
        name="merge",
    )(ya, yb, zg, x, gt, g, wa, wb, wo)


def _ffn_kernel(x_ref, sc_ref, sh_ref, gt_ref, gpre_ref, gpost_ref, up_ref, down_ref, o_ref):
    x = x_ref[...]
    h = (_rms(x, gpre_ref[...]) * (1.0 + sc_ref[0]) + sh_ref[0]).astype(BF16)
    f = jnp.zeros(x.shape, F32)
    for j in range(FFN_HIDDEN // FFN_SLAB):
        sl = slice(j * FFN_SLAB, (j + 1) * FFN_SLAB)
        a = jnp.maximum(jnp.dot(h, up_ref[:, sl], preferred_element_type=F32), 0.0)
        f = f + _dot(a * a, down_ref[sl, :])
    o_ref[...] = x + gt_ref[0] * _rms(f, gpost_ref[...])


def _ffn(x, sc, sh, gt, gpre, gpost, up, down, tiles_per_group):
    n, d = x.shape
    rows = pl.BlockSpec((ROW_TILE, d), lambda i: (i, 0))
    mod = _mod_spec(sc, tiles_per_group)
    return pl.pallas_call(
        _ffn_kernel,
        grid=(n // ROW_TILE,),
        in_specs=[rows, mod, mod, mod, _resident((1, d)), _resident((1, d)),
                  _resident(up.shape), _resident(down.shape)],
        out_specs=rows,
        out_shape=jax.ShapeDtypeStruct((n, d), F32),
        compiler_params=_params("arbitrary"),
        name="ffn",
    )(x, sc, sh, gt, gpre, gpost, up, down)


def _pad_frames(z, bsz, t, t_pad):
    if t_pad == t:
        return z
    z = z.reshape(bsz, t, -1)
    return jnp.pad(z, ((0, 0), (0, t_pad - t), (0, 0))).reshape(bsz * t_pad, -1)


def _drop_frames(y, bsz, t, t_pad):
    if t_pad == t:
        return y
    return y.reshape(bsz, t_pad, -1)[:, :t].reshape(bsz * t, -1)


def _layer(x, mod, shift0, sp0, ret0, pos0, pr, bsz, t):
    n = bsz * t
    if t % ROW_TILE == 0:
        tiles_per_group = t // ROW_TILE
        mods = [m.reshape(bsz, 1, D_MODEL) for m in jnp.split(mod, 6, axis=-1)]
    else:
        tiles_per_group = 1
        mods = [jnp.repeat(m, t, axis=0).reshape(n // ROW_TILE, ROW_TILE, D_MODEL)
                for m in jnp.split(mod, 6, axis=-1)]
    sh1, sc1, gt1, sh2, sc2, gt2 = mods

    h = _prenorm(x, pr["norm_pre0"], sc1, sh1, tiles_per_group)
    za = _matmul(h, pr["w_in_a"], A_COLS // 2, "in_proj_a")
    zb = _matmul(h, pr["w_in_b"], B_COLS // 4, "in_proj_b")
    zg = _matmul(h, pr["w_in_g"], GATE_COLS, "in_proj_g")

    ca = min(t, A_CHUNK)
    ta = -(-t // A_CHUNK) * A_CHUNK
    ya, sp, shift = _rwkv(_pad_frames(za, bsz, t, ta), shift0, sp0, pr, ta // A_CHUNK, ca)
    ya = _drop_frames(ya, bsz, t, ta)

    cb = min(t, B_CHUNK)
    tb = -(-t // B_CHUNK) * B_CHUNK
    pos = (pos0 + jnp.arange(tb, dtype=jnp.int32)).astype(F32)
    half = B_KEY_DIM // 2
    inv_freq = ROPE_BASE ** (-jnp.linspace(0.0, 1.0, half, dtype=F32))
    ang = pos[:, None] * inv_freq[None, :]
    yb, ret = _ret(_pad_frames(zb, bsz, t, tb), jnp.cos(ang), jnp.sin(ang), ret0, tb // B_CHUNK, cb)
    yb = _drop_frames(yb, bsz, t, tb)

    x = _merge(ya, yb, zg, x, gt1, pr["norm_post0"], pr["w_branch_a"], pr["w_branch_b"], pr["w_out"],
               tiles_per_group)
    x = _ffn(x, sc2, sh2, gt2, pr["norm_pre1"], pr["norm_post1"], pr["w_ffn_up"], pr["w_ffn_down"],
             tiles_per_group)
    return x, shift.reshape(bsz, A_COLS), _unpack_pairs(sp), ret


def kernel(x_prompt, x_sample, state_shift, state_rwkv, state_ret, c_prompt, c_sample, ada_w, ada_b, norm_pre, norm_post, w_in, a_mu, a_w0, a_w2, a_a0, a_a2, a_g2, a_kk, a_ka, a_rk, a_ln_w, a_ln_b, w_branch_a, w_branch_b, w_out, w_ffn_up, w_ffn_down):
    depth = w_in.shape[0]
    bp, tp, _ = x_prompt.shape
    bs, ts, _ = x_sample.shape
    xp = x_prompt.reshape(bp * tp, D_MODEL)
    xs = x_sample.reshape(bs * ts, D_MODEL)
    c_all = jnp.concatenate([c_prompt, c_sample], axis=0)
    row = lambda a: a.reshape(1, -1)
    lora_pad = jnp.zeros((A_DECAY_RANK, A_WIDTH), BF16)

    zero_shift = jnp.zeros((bp, 1, A_COLS), F32)
    zero_sp = jnp.zeros((bp, A_PAIRS, LANES, LANES), F32)
    zero_ret = jnp.zeros((bp, B_HEADS, B_KEY_DIM, B_VAL_DIM), F32)

    outs = [[] for _ in range(6)]
    for l in range(depth):
        wl = w_in[l].astype(BF16)
        pr = {
            "norm_pre0": row(norm_pre[l, 0]), "norm_pre1": row(norm_pre[l, 1]),
            "norm_post0": row(norm_post[l, 0]), "norm_post1": row(norm_post[l, 1]),
            "w_in_a": wl[:, :A_COLS], "w_in_b": wl[:, A_COLS:A_COLS + B_COLS],
            "w_in_g": wl[:, A_COLS + B_COLS:],
            "a_mu": row(a_mu[l]), "a_w0": row(a_w0[l]), "a_a0": row(a_a0[l]),
            "a_w2p": jnp.concatenate([a_w2[l].astype(BF16), lora_pad], axis=0),
            "a_a2p": jnp.concatenate([lora_pad, a_a2[l].astype(BF16)], axis=0),
            "a_g2": a_g2[l].astype(BF16),
            "a_kk": row(a_kk[l]), "a_ka": row(a_ka[l]), "a_rk": row(a_rk[l]),
            "a_ln_w": row(a_ln_w[l]), "a_ln_b": row(a_ln_b[l]),
            "w_branch_a": w_branch_a[l].astype(BF16), "w_branch_b": w_branch_b[l].astype(BF16),
            "w_out": w_out[l].astype(BF16),
            "w_ffn_up": w_ffn_up[l].astype(BF16), "w_ffn_down": w_ffn_down[l].astype(BF16),
        }
        mod = _ada(c_all, ada_w[l].astype(BF16), ada_b[l])
        xp, sp_, rp_, tp_ = _layer(xp, mod[:bp], zero_shift, zero_sp, zero_ret, 0, pr, bp, tp)
        xs, ss_, rs_, ts_ = _layer(xs, mod[bp:], state_shift[l].reshape(bs, 1, A_COLS),
                                   _pack_pairs(state_rwkv[l]), state_ret[l], PAST_LEN, pr, bs, ts)
        for lst, val in zip(outs, (sp_, rp_, tp_, ss_, rs_, ts_)):
            lst.append(val)

    stacked = [jnp.stack(o) for o in outs]
    return (xp.reshape(bp, tp, D_MODEL), xs.reshape(bs, ts, D_MODEL), *stacked)
```

```python
import functools
import math

import jax
import jax.numpy as jnp
from jax import lax
from jax.experimental import pallas as pl
from jax.experimental.pallas import tpu as pltpu

F32 = jnp.float32
BF16 = jnp.bfloat16

D_MODEL = 1024
A_HEADS = 16
A_HEAD_DIM = 64
A_WIDTH = A_HEADS * A_HEAD_DIM
A_DECAY_RANK = 64
A_ICLR_RANK = 64
A_GATE_RANK = 128
A_COLS = 3 * A_WIDTH + A_DECAY_RANK + A_ICLR_RANK + A_GATE_RANK
A_GN_EPS = 64e-5
B_HEADS = 4
B_KEY_DIM = 256
B_VAL_DIM = 512
B_QK_WIDTH = B_HEADS * B_KEY_DIM
B_V_WIDTH = B_HEADS * B_VAL_DIM
B_COLS = 2 * B_QK_WIDTH + 2 * B_V_WIDTH
GATE_COLS = 2 * D_MODEL
FFN_HIDDEN = 4 * D_MODEL
ROPE_BASE = 10000.0
NORM_EPS = 1e-6
PAST_LEN = 4096

LANES = 128
MXU_DIM = 256
A_PAIRS = A_WIDTH // LANES
A_CHUNK = 64
A_SUBCHUNKS = 4
B_CHUNK = 128
ROW_TILE = 512
FFN_SLAB = 1024
VMEM_LIMIT = 56 * 1024 * 1024


def _dot(a, b):
    return jnp.dot(a.astype(BF16), b.astype(BF16), preferred_element_type=F32)


def _dot_nt(a, b):
    return lax.dot_general(a.astype(BF16), b.astype(BF16), (((1,), (1,)), ((), ())),
                           preferred_element_type=F32)


def _rms(x, g):
    return x * lax.rsqrt(jnp.mean(x * x, axis=-1, keepdims=True) + NORM_EPS) * g


def _params(*sem):
    return pltpu.CompilerParams(dimension_semantics=sem, vmem_limit_bytes=VMEM_LIMIT)


def _ada_kernel(c_ref, w_ref, b_ref, o_ref):
    c = c_ref[...]
    o_ref[...] = _dot(c * jax.nn.sigmoid(c), w_ref[...]) + b_ref[...]


def _ada(c, w, b):
    n, d = c.shape
    cols = w.shape[1]
    tn = cols // 4
    return pl.pallas_call(
        _ada_kernel,
        grid=(cols // tn,),
        in_specs=[pl.BlockSpec((n, d), lambda j: (0, 0)),
                  pl.BlockSpec((d, tn), lambda j: (0, j)),
                  pl.BlockSpec((1, tn), lambda j: (0, j))],
        out_specs=pl.BlockSpec((n, tn), lambda j: (0, j)),
        out_shape=jax.ShapeDtypeStruct((n, cols), F32),
        compiler_params=_params("arbitrary"),
        name="ada",
    )(c, w, b.reshape(1, cols))


def _prenorm_kernel(x_ref, g_ref, sc_ref, sh_ref, o_ref):
    h = _rms(x_ref[...], g_ref[...]) * (1.0 + sc_ref[0]) + sh_ref[0]
    o_ref[...] = h.astype(o_ref.dtype)


def _mod_spec(mod, tiles_per_group):
    _, r, d = mod.shape
    return pl.BlockSpec((1, r, d), lambda i: (i // tiles_per_group, 0, 0))


def _prenorm(x, g, sc, sh, tiles_per_group):
    n, d = x.shape
    return pl.pallas_call(
        _prenorm_kernel,
        grid=(n // ROW_TILE,),
        in_specs=[pl.BlockSpec((ROW_TILE, d), lambda i: (i, 0)),
                  pl.BlockSpec((1, d), lambda i: (0, 0)),
                  _mod_spec(sc, tiles_per_group), _mod_spec(sh, tiles_per_group)],
        out_specs=pl.BlockSpec((ROW_TILE, d), lambda i: (i, 0)),
        out_shape=jax.ShapeDtypeStruct((n, d), BF16),
        compiler_params=_params("arbitrary"),
        name="prenorm",
    )(x, g, sc, sh)


def _matmul_kernel(a_ref, w_ref, o_ref):
    o_ref[...] = jnp.dot(a_ref[...], w_ref[...], preferred_element_type=F32).astype(o_ref.dtype)


def _matmul(a, w, tn, name):
    n, k = a.shape
    cols = w.shape[1]
    tm = min(n, 1024)
    return pl.pallas_call(
        _matmul_kernel,
        grid=(cols // tn, n // tm),
        in_specs=[pl.BlockSpec((tm, k), lambda j, i: (i, 0)),
                  pl.BlockSpec((k, tn), lambda j, i: (0, j))],
        out_specs=pl.BlockSpec((tm, tn), lambda j, i: (i, j)),
        out_shape=jax.ShapeDtypeStruct((n, cols), BF16),
        compiler_params=_params("arbitrary", "arbitrary"),
        name=name,
    )(a, w)


def _split3(x):
    h1 = x.astype(BF16)
    r1 = x - h1.astype(F32)
    h2 = r1.astype(BF16)
    h3 = (r1 - h2.astype(F32)).astype(BF16)
    return h1, h2, h3


def _head_sum(x, ones_bd):
    c = x.shape[0]
    groups = A_WIDTH // MXU_DIM
    xs = jnp.concatenate([x[:, q * MXU_DIM:(q + 1) * MXU_DIM] for q in range(groups)], axis=0)
    s = jnp.dot(xs.astype(BF16), ones_bd, preferred_element_type=F32)
    return jnp.concatenate([s[q * c:(q + 1) * c] for q in range(groups)], axis=1)


def _rwkv_kernel(*refs, valid_rows, n_sub, zero_init, n_alias):
    if zero_init:
        za_ref = refs[0]
        refs = refs[1:]
    else:
        za_ref, shift0_ref, sp0_ref = refs[:3]
        refs = refs[3:]
    (mu_ref, w0_ref, w2_ref, a0_ref, a2_ref, g2_ref, kkp_ref, kap_ref, rk_ref, lnw_ref,
     lnb_ref) = refs[:11]
    ya_ref, sp_ref, shift_out_ref, carry_ref = refs[11 + n_alias:]
    C = A_CHUNK
    C2 = 2 * C
    R = n_sub * C
    c = pl.program_id(1)

    @pl.when(c == 0)
    def _():
        if zero_init:
            carry_ref[...] = jnp.zeros(carry_ref.shape, F32)
            sp_ref[...] = jnp.zeros(sp_ref.shape, F32)
        else:
            carry_ref[...] = shift0_ref[0]
            sp_ref[...] = sp0_ref[0]

    za = za_ref[...].astype(F32)
    row = lax.broadcasted_iota(jnp.int32, (R, 1), 0)
    za_prev = jnp.where(row == 0, carry_ref[...], pltpu.roll(za, 1, axis=0))
    carry_ref[...] = za[R - 1:R, :]
    last = R - C + valid_rows - 1
    shift_out_ref[...] = za[last:last + 1, :]
    mix = za + (za_prev - za) * mu_ref[...]

    r = mix[:, 0:A_WIDTH]
    k = mix[:, A_WIDTH:2 * A_WIDTH]
    v = mix[:, 2 * A_WIDTH:3 * A_WIDTH]
    zwi = mix[:, 3 * A_WIDTH:3 * A_WIDTH + LANES]
    zg = mix[:, 3 * A_WIDTH + LANES:A_COLS]

    lw = -math.exp(-0.5) * jax.nn.sigmoid(w0_ref[...] + _dot(jnp.tanh(zwi), w2_ref[...]))
    iclr = jax.nn.sigmoid(a0_ref[...] + _dot(zwi, a2_ref[...]))
    gate = _dot(jax.nn.sigmoid(zg), g2_ref[...])

    li = lax.broadcasted_iota(jnp.int32, (LANES, LANES), 0)
    lj = lax.broadcasted_iota(jnp.int32, (LANES, LANES), 1)
    blk_diag = (li // A_HEAD_DIM) == (lj // A_HEAD_DIM)
    mi = lax.broadcasted_iota(jnp.int32, (MXU_DIM, MXU_DIM), 0)
    mj = lax.broadcasted_iota(jnp.int32, (MXU_DIM, MXU_DIM), 1)
    ones_bd = jnp.where((mi // A_HEAD_DIM) == (mj // A_HEAD_DIM), 1.0, 0.0).astype(BF16)

    kk = k * kkp_ref[...]
    kk = kk * lax.rsqrt(jnp.maximum(_head_sum(kk * kk, ones_bd), 1e-24))
    k2 = k * (1.0 + (iclr - 1.0) * kap_ref[...])
    if valid_rows < C:
        valid = row < valid_rows
        lw = jnp.where(valid, lw, 0.0)
        kk = jnp.where(valid, kk, 0.0)
        k2 = jnp.where(valid, k2, 0.0)
    b = kk * iclr

    ti = lax.broadcasted_iota(jnp.int32, (R, R), 0)
    tj = lax.broadcasted_iota(jnp.int32, (R, R), 1)
    tril_incl = jnp.where((ti >= tj) & ((ti // C) == (tj // C)), 1.0, 0.0).astype(BF16)
    cum = sum(jnp.dot(tril_incl, part, preferred_element_type=F32) for part in _split3(lw))
    e_p = jnp.exp(cum)
    e_m = jnp.exp(-cum)
    e_tots = [e_p[j * C + C - 1:j * C + C, :] for j in range(n_sub)]
    e_rem = jnp.concatenate([e_tots[j] * e_m[j * C:(j + 1) * C] for j in range(n_sub)], axis=0)
    rt = r * e_p
    at = -kk * jnp.exp(cum - lw)
    bt = b * e_m
    kt = k2 * e_m
    kh = k2 * e_rem
    bh = b * e_rem

    gi = lax.broadcasted_iota(jnp.int32, (2 * C2, C2), 0)
    gj = lax.broadcasted_iota(jnp.int32, (2 * C2, C2), 1)
    blk = gi // C
    lower = (gi % C) >= (gj % C)
    mask_g = lower & (((blk == 1) | (blk == 2)) | ((gi % C) != (gj % C)))
    qi = lax.broadcasted_iota(jnp.int32, (C2, C2), 0)
    qj = lax.broadcasted_iota(jnp.int32, (C2, C2), 1)
    blk_tl = (qi < C) & (qj < C)
    blk_br = (qi >= C) & (qj >= C)
    eye2 = jnp.where(qi == qj, 1.0, 0.0).astype(F32)
    lane = lax.broadcasted_iota(jnp.int32, (1, LANES), 1)
    m0 = lane < A_HEAD_DIM
    zeros = jnp.zeros((C, LANES), F32)
    n_double = int(math.log2(C)) - 1

    pairs = range(A_PAIRS)
    lanes = [slice(p * LANES, (p + 1) * LANES) for p in pairs]
    rows = [slice(j * C, (j + 1) * C) for j in range(n_sub)]
    streams = [(rw, ln) for rw in rows for ln in lanes]
    cat = lambda *xs: jnp.concatenate(xs, axis=0)
    only0 = lambda x: jnp.where(m0, x, 0.0)
    only1 = lambda x: jnp.where(m0, 0.0, x)
    split01 = lambda x: only0(x[0:C]) + only1(x[C:C2])
    gs = [jnp.where(mask_g, _dot(cat(only0(at[s]), only0(rt[s]), only1(rt[s]), only1(at[s])),
                                 cat(bt[s], kt[s]).T), 0.0) for s in streams]

    pws = [jnp.where(blk_tl, g[0:C2], 0.0) + jnp.where(blk_br, pltpu.roll(g[C2:2 * C2], C, axis=1), 0.0)
           for g in gs]
    tts = [eye2 + nn for nn in pws]
    pws = [_dot(pw, pw) for pw in pws]
    for _ in range(n_double - 1):
        both = [_dot(cat(pw, tt), pw) for pw, tt in zip(pws, tts)]
        pws = [bth[0:C2] for bth in both]
        tts = [tt + bth[C2:2 * C2] for tt, bth in zip(tts, both)]
    tts = [tt + _dot(tt, pw) for tt, pw in zip(tts, pws)]
    t01s = [tt[0:C] + tt[C:C2] for tt in tts]
    akv = [split01(_dot(cat(g[0:C], g[3 * C:4 * C]), cat(zeros, v[s]))) for g, s in zip(gs, streams)]

    sps = [sp_ref[p] for p in pairs]
    ys = []
    for j in range(n_sub):
        sel = slice(j * A_PAIRS, (j + 1) * A_PAIRS)
        st = streams[sel]
        from_state = [_dot(cat(at[s], rt[s]), sp.T) for s, sp in zip(st, sps)]
        ws = [fs[0:C] + x for fs, x in zip(from_state, akv[sel])]
        us = [_dot(t01, cat(only0(w), only1(w))) for t01, w in zip(t01s[sel], ws)]
        ys.append([fs[C:C2] + split01(_dot(g[C:3 * C], cat(u, v[s])))
                   for fs, g, s, u in zip(from_state, gs[sel], st, us)])
        sps = [jnp.where(blk_diag, sp * e_tots[j][:, s[1]] + _dot(cat(v[s], u).T, cat(kh[s], bh[s])), 0.0)
               for s, sp, u in zip(st, sps, us)]
    for p in pairs:
        sp_ref[p] = sps[p]

    y = jnp.concatenate([jnp.concatenate(yj, axis=1) for yj in ys], axis=0)
    mu = _head_sum(y, ones_bd) * (1.0 / A_HEAD_DIM)
    d = y - mu
    var = _head_sum(d * d, ones_bd) * (1.0 / A_HEAD_DIM)
    yn = d * lax.rsqrt(var + A_GN_EPS) * lnw_ref[...] + lnb_ref[...]
    bonus = _head_sum(r * k2 * rk_ref[...], ones_bd) * v
    ya_ref[...] = ((yn + bonus) * gate).astype(ya_ref.dtype)


def _stacked_out(layer, depth, bsz, tail):
    spec = pl.BlockSpec((None, None) + tail, lambda b, c: (layer, b) + (0,) * len(tail))
    return spec, jax.ShapeDtypeStruct((depth, bsz) + tail, F32)


def _rwkv(za, shift0, sp0, pr, bsz, n_steps, n_sub, valid_rows, layer, depth, prev):
    R = n_sub * A_CHUNK
    zero_init = shift0 is None
    vec = lambda n: pl.BlockSpec((1, n), lambda b, c: (0, 0))
    mat = lambda r, n: pl.BlockSpec((r, n), lambda b, c: (0, 0))
    in_specs = [pl.BlockSpec((R, A_COLS), lambda b, c: (b * n_steps + c, 0))]
    args = [za]
    if not zero_init:
        in_specs += [pl.BlockSpec((1, 1, A_COLS), lambda b, c: (b, 0, 0)),
                     pl.BlockSpec((1, A_PAIRS, LANES, LANES), lambda b, c: (b, 0, 0, 0))]
        args += [shift0, sp0]
    in_specs += [vec(A_COLS), vec(A_WIDTH), mat(LANES, A_WIDTH), vec(A_WIDTH), mat(LANES, A_WIDTH),
                 mat(A_GATE_RANK, A_WIDTH), vec(A_WIDTH), vec(A_WIDTH), vec(A_WIDTH),
                 vec(A_WIDTH), vec(A_WIDTH)]
    args += [pr["a_mu"], pr["a_w0"], pr["a_w2p"], pr["a_a0"], pr["a_a2p"], pr["a_g2"],
             pr["a_kk"], pr["a_ka"], pr["a_rk"], pr["a_ln_w"], pr["a_ln_b"]]
    aliases = {}
    if prev is not None:
        aliases = {len(args): 1, len(args) + 1: 2}
        in_specs += [pl.BlockSpec(memory_space=pl.ANY)] * 2
        args += list(prev)
    sp_spec, sp_shape = _stacked_out(layer, depth, bsz, (A_PAIRS, LANES, LANES))
    sh_spec, sh_shape = _stacked_out(layer, depth, bsz, (1, A_COLS))
    ya, sp_all, shift_all = pl.pallas_call(
        functools.partial(_rwkv_kernel, valid_rows=valid_rows, n_sub=n_sub, zero_init=zero_init,
                          n_alias=len(aliases)),
        grid=(bsz, n_steps),
        in_specs=in_specs,
        out_specs=[pl.BlockSpec((R, A_WIDTH), lambda b, c: (b * n_steps + c, 0)), sp_spec, sh_spec],
        out_shape=[jax.ShapeDtypeStruct((bsz * n_steps * R, A_WIDTH), BF16), sp_shape, sh_shape],
        scratch_shapes=[pltpu.VMEM((1, A_COLS), F32)],
        input_output_aliases=aliases,
        compiler_params=_params("arbitrary", "arbitrary"),
        name="rwkv",
    )(*args)
    return ya, (sp_all, shift_all)


def _pack_pairs(s):
    bsz = s.shape[0]
    s = s.reshape(bsz, A_PAIRS, 2, A_HEAD_DIM, A_HEAD_DIM)
    z = jnp.zeros_like(s[:, :, 0])
    top = jnp.concatenate([s[:, :, 0], z], axis=-1)
    bot = jnp.concatenate([z, s[:, :, 1]], axis=-1)
    return jnp.concatenate([top, bot], axis=-2)


def _unpack_pairs(sp):
    h = A_HEAD_DIM
    s = jnp.stack([sp[..., :h, :h], sp[..., h:, h:]], axis=-3)
    return s.reshape(sp.shape[:-3] + (A_HEADS, h, h))


def _ret_kernel(*refs, valid_rows, zero_init, n_alias):
    if zero_init:
        zb_ref, cos_ref, sin_ref = refs[:3]
    else:
        zb_ref, cos_ref, sin_ref, s0_ref = refs[:4]
    yb_ref, s_ref = refs[-2:]
    C = B_CHUNK
    c = pl.program_id(1)

    @pl.when(c == 0)
    def _():
        if zero_init:
            s_ref[...] = jnp.zeros(s_ref.shape, F32)
        else:
            s_ref[...] = s0_ref[0]

    cos = cos_ref[...]
    sin = sin_ref[...]
    half = B_KEY_DIM // 2
    ri = lax.broadcasted_iota(jnp.int32, (C, C), 0)
    rj = lax.broadcasted_iota(jnp.int32, (C, C), 1)
    diff = (ri - rj).astype(F32)
    idx = lax.broadcasted_iota(jnp.int32, (C, 1), 0).astype(F32)

    def rope(x):
        x1, x2 = x[:, :half], x[:, half:]
        return jnp.concatenate([x1 * cos - x2 * sin, x1 * sin + x2 * cos], axis=1)

    for h in range(B_HEADS):
        log_g = math.log(1.0 - 2.0 ** (-5.0 - h))
        q = rope(zb_ref[:, h * B_KEY_DIM:(h + 1) * B_KEY_DIM].astype(F32))
        k = rope(zb_ref[:, B_QK_WIDTH + h * B_KEY_DIM:B_QK_WIDTH + (h + 1) * B_KEY_DIM].astype(F32))
        k = k * (B_KEY_DIM ** -0.5)
        v = zb_ref[:, 2 * B_QK_WIDTH + h * B_VAL_DIM:2 * B_QK_WIDTH + (h + 1) * B_VAL_DIM]
        g = zb_ref[:, 2 * B_QK_WIDTH + B_V_WIDTH + h * B_VAL_DIM:
                   2 * B_QK_WIDTH + B_V_WIDTH + (h + 1) * B_VAL_DIM].astype(F32)
        mask = jnp.where(diff >= 0.0, jnp.exp(jnp.maximum(diff, 0.0) * log_g), 0.0)
        scores = _dot_nt(q, k) * mask
        s = s_ref[h]
        y = _dot(scores, v) + _dot(q * jnp.exp((idx + 1.0) * log_g), s)
        k_dec = jnp.where(idx < valid_rows, k * jnp.exp((valid_rows - 1.0 - idx) * log_g), 0.0)
        s_ref[h] = s * math.exp(valid_rows * log_g) + _dot(k_dec.T, v)
        yn = y * lax.rsqrt(jnp.mean(y * y, axis=-1, keepdims=True) + NORM_EPS)
        yb_ref[:, h * B_VAL_DIM:(h + 1) * B_VAL_DIM] = (g * jax.nn.sigmoid(g) * yn).astype(yb_ref.dtype)


def _ret(zb, cos, sin, s0, bsz, n_chunks, valid_rows, layer, depth, prev):
    C = B_CHUNK
    half = B_KEY_DIM // 2
    zero_init = s0 is None
    in_specs = [pl.BlockSpec((C, B_COLS), lambda b, c: (b * n_chunks + c, 0)),
                pl.BlockSpec((C, half), lambda b, c: (c, 0)),
                pl.BlockSpec((C, half), lambda b, c: (c, 0))]
    args = [zb, cos, sin]
    if not zero_init:
        in_specs.append(pl.BlockSpec((1, B_HEADS, B_KEY_DIM, B_VAL_DIM), lambda b, c: (b, 0, 0, 0)))
        args.append(s0)
    aliases = {}
    if prev is not None:
        aliases = {len(args): 1}
        in_specs.append(pl.BlockSpec(memory_space=pl.ANY))
        args.append(prev)
    s_spec, s_shape = _stacked_out(layer, depth, bsz, (B_HEADS, B_KEY_DIM, B_VAL_DIM))
    return pl.pallas_call(
        functools.partial(_ret_kernel, valid_rows=valid_rows, zero_init=zero_init, n_alias=len(aliases)),
        grid=(bsz, n_chunks),
        in_specs=in_specs,
        out_specs=[pl.BlockSpec((C, B_V_WIDTH), lambda b, c: (b * n_chunks + c, 0)), s_spec],
        out_shape=[jax.ShapeDtypeStruct((bsz * n_chunks * C, B_V_WIDTH), BF16), s_shape],
        input_output_aliases=aliases,
        compiler_params=_params("arbitrary", "arbitrary"),
        name="ret",
    )(*args)


def _merge_kernel(ya_ref, yb_ref, zg_ref, x_ref, gt_ref, g_ref, wa_ref, wb_ref, wo_ref, o_ref):
    pa = jnp.dot(ya_ref[...], wa_ref[...], preferred_element_type=F32)
    pb = jnp.dot(yb_ref[...], wb_ref[...], preferred_element_type=F32)
    ga = jax.nn.sigmoid(zg_ref[:, :D_MODEL].astype(F32))
    gb = jax.nn.sigmoid(zg_ref[:, D_MODEL:].astype(F32))
    out = _dot(ga * pa + gb * pb, wo_ref[...])
    o_ref[...] = x_ref[...] + gt_ref[0] * _rms(out, g_ref[...])


def _resident(shape):
    return pl.BlockSpec(shape, lambda i: (0,) * len(shape), pipeline_mode=pl.Buffered(1))


def _merge(ya, yb, zg, x, gt, g, wa, wb, wo, tiles_per_group):
    n, d = x.shape
    rows = lambda w: pl.BlockSpec((ROW_TILE, w), lambda i: (i, 0))
    return pl.pallas_call(
        _merge_kernel,
        grid=(n // ROW_TILE,),
        in_specs=[rows(A_WIDTH), rows(B_V_WIDTH), rows(GATE_COLS), rows(d),
                  _mod_spec(gt, tiles_per_group), _resident((1, d)),
                  _resident(wa.shape), _resident(wb.shape), _resident(wo.shape)],
        out_specs=rows(d),
        out_shape=jax.ShapeDtypeStruct((n, d), F32),
        compiler_params=_params("arbitrary"),
        name="merge",
    )(ya, yb, zg, x, gt, g, wa, wb, wo)


def _ffn_kernel(x_ref, sc_ref, sh_ref, gt_ref, gpre_ref, gpost_ref, up_ref, down_ref, o_ref):
    x = x_ref[...]
    h = (_rms(x, gpre_ref[...]) * (1.0 + sc_ref[0]) + sh_ref[0]).astype(BF16)
    f = jnp.zeros(x.shape, F32)
    for j in range(FFN_HIDDEN // FFN_SLAB):
        sl = slice(j * FFN_SLAB, (j + 1) * FFN_SLAB)
        a = jnp.maximum(jnp.dot(h, up_ref[:, sl], preferred_element_type=F32), 0.0)
        f = f + _dot(a * a, down_ref[sl, :])
    o_ref[...] = x + gt_ref[0] * _rms(f, gpost_ref[...])


def _ffn(x, sc, sh, gt, gpre, gpost, up, down, tiles_per_group):
    n, d = x.shape
    rows = pl.BlockSpec((ROW_TILE, d), lambda i: (i, 0))
    mod = _mod_spec(sc, tiles_per_group)
    return pl.pallas_call(
        _ffn_kernel,
        grid=(n // ROW_TILE,),
        in_specs=[rows, mod, mod, mod, _resident((1, d)), _resident((1, d)),
                  _resident(up.shape), _resident(down.shape)],
        out_specs=rows,
        out_shape=jax.ShapeDtypeStruct((n, d), F32),
        compiler_params=_params("arbitrary"),
        name="ffn",
    )(x, sc, sh, gt, gpre, gpost, up, down)


def _pad_frames(z, bsz, t, t_pad):
    if t_pad == t:
        return z
    z = z.reshape(bsz, t, -1)
    return jnp.pad(z, ((0, 0), (0, t_pad - t), (0, 0))).reshape(bsz * t_pad, -1)


def _drop_frames(y, bsz, t, t_pad):
    if t_pad == t:
        return y
    return y.reshape(bsz, t_pad, -1)[:, :t].reshape(bsz * t, -1)


def _layer(x, mod, shift0, sp0, ret0, pos0, pr, bsz, t, layer, depth, prev):
    n = bsz * t
    if t % ROW_TILE == 0:
        tiles_per_group = t // ROW_TILE
        mods = [m.reshape(bsz, 1, D_MODEL) for m in jnp.split(mod, 6, axis=-1)]
    else:
        tiles_per_group = 1
        mods = [jnp.repeat(m, t, axis=0).reshape(n // ROW_TILE, ROW_TILE, D_MODEL)
                for m in jnp.split(mod, 6, axis=-1)]
    sh1, sc1, gt1, sh2, sc2, gt2 = mods

    h = _prenorm(x, pr["norm_pre0"], sc1, sh1, tiles_per_group)
    za = _matmul(h, pr["w_in_a"], A_COLS // 2, "in_proj_a")
    zb = _matmul(h, pr["w_in_b"], B_COLS // 4, "in_proj_b")
    zg = _matmul(h, pr["w_in_g"], GATE_COLS, "in_proj_g")

    n_sub = A_SUBCHUNKS if t % (A_SUBCHUNKS * A_CHUNK) == 0 else 1
    ra = n_sub * A_CHUNK
    ta = -(-t // ra) * ra
    ya, rwkv_states = _rwkv(_pad_frames(za, bsz, t, ta), shift0, sp0, pr, bsz, ta // ra, n_sub,
                            min(t, A_CHUNK), layer, depth, None if prev is None else prev[0])
    ya = _drop_frames(ya, bsz, t, ta)

    tb = -(-t // B_CHUNK) * B_CHUNK
    pos = (pos0 + jnp.arange(tb, dtype=jnp.int32)).astype(F32)
    half = B_KEY_DIM // 2
    inv_freq = ROPE_BASE ** (-jnp.linspace(0.0, 1.0, half, dtype=F32))
    ang = pos[:, None] * inv_freq[None, :]
    yb, ret_states = _ret(_pad_frames(zb, bsz, t, tb), jnp.cos(ang), jnp.sin(ang), ret0, bsz,
                          tb // B_CHUNK, min(t, B_CHUNK), layer, depth, None if prev is None else prev[1])
    yb = _drop_frames(yb, bsz, t, tb)

    x = _merge(ya, yb, zg, x, gt1, pr["norm_post0"], pr["w_branch_a"], pr["w_branch_b"], pr["w_out"],
               tiles_per_group)
    x = _ffn(x, sc2, sh2, gt2, pr["norm_pre1"], pr["norm_post1"], pr["w_ffn_up"], pr["w_ffn_down"],
             tiles_per_group)
    return x, (rwkv_states, ret_states)


def kernel(x_prompt, x_sample, state_shift, state_rwkv, state_ret, c_prompt, c_sample, ada_w, ada_b, norm_pre, norm_post, w_in, a_mu, a_w0, a_w2, a_a0, a_a2, a_g2, a_kk, a_ka, a_rk, a_ln_w, a_ln_b, w_branch_a, w_branch_b, w_out, w_ffn_up, w_ffn_down):
    depth = w_in.shape[0]
    bp, tp, _ = x_prompt.shape
    bs, ts, _ = x_sample.shape
    xp = x_prompt.reshape(bp * tp, D_MODEL)
    xs = x_sample.reshape(bs * ts, D_MODEL)
    c_all = jnp.concatenate([c_prompt, c_sample], axis=0)
    row = lambda a: a.reshape(1, -1)
    lora_pad = jnp.zeros((A_DECAY_RANK, A_WIDTH), BF16)

    prev_p = prev_s = None
    for l in range(depth):
        wl = w_in[l].astype(BF16)
        pr = {
            "norm_pre0": row(norm_pre[l, 0]), "norm_pre1": row(norm_pre[l, 1]),
            "norm_post0": row(norm_post[l, 0]), "norm_post1": row(norm_post[l, 1]),
            "w_in_a": wl[:, :A_COLS], "w_in_b": wl[:, A_COLS:A_COLS + B_COLS],
            "w_in_g": wl[:, A_COLS + B_COLS:],
            "a_mu": row(a_mu[l]), "a_w0": row(a_w0[l]), "a_a0": row(a_a0[l]),
            "a_w2p": jnp.concatenate([a_w2[l].astype(BF16), lora_pad], axis=0),
            "a_a2p": jnp.concatenate([lora_pad, a_a2[l].astype(BF16)], axis=0),
            "a_g2": a_g2[l].astype(BF16),
            "a_kk": row(a_kk[l]), "a_ka": row(a_ka[l]), "a_rk": row(a_rk[l]),
            "a_ln_w": row(a_ln_w[l]), "a_ln_b": row(a_ln_b[l]),
            "w_branch_a": w_branch_a[l].astype(BF16), "w_branch_b": w_branch_b[l].astype(BF16),
            "w_out": w_out[l].astype(BF16),
            "w_ffn_up": w_ffn_up[l].astype(BF16), "w_ffn_down": w_ffn_down[l].astype(BF16),
        }
        mod = _ada(c_all, ada_w[l].astype(BF16), ada_b[l])
        xp, prev_p = _layer(xp, mod[:bp], None, None, None, 0, pr, bp, tp, l, depth, prev_p)
        xs, prev_s = _layer(xs, mod[bp:], state_shift[l].reshape(bs, 1, A_COLS),
                            _pack_pairs(state_rwkv[l]), state_ret[l], PAST_LEN, pr, bs, ts, l, depth, prev_s)

    def states(prev, bsz):
        (sp_all, shift_all), ret_all = prev
        return shift_all.reshape(depth, bsz, A_COLS), _unpack_pairs(sp_all), ret_all

    return (xp.reshape(bp, tp, D_MODEL), xs.reshape(bs, ts, D_MODEL), *states(prev_p, bp), *states(prev_s, bs))
```

```python
import functools
import math

import jax
import jax.numpy as jnp
from jax import lax
from jax.experimental import pallas as pl
from jax.experimental.pallas import tpu as pltpu

F32 = jnp.float32
BF16 = jnp.bfloat16

D_MODEL = 1024
A_HEADS = 16
A_HEAD_DIM = 64
A_WIDTH = A_HEADS * A_HEAD_DIM
A_DECAY_RANK = 64
A_ICLR_RANK = 64
A_GATE_RANK = 128
A_COLS = 3 * A_WIDTH + A_DECAY_RANK + A_ICLR_RANK + A_GATE_RANK
A_GN_EPS = 64e-5
B_HEADS = 4
B_KEY_DIM = 256
B_VAL_DIM = 512
B_QK_WIDTH = B_HEADS * B_KEY_DIM
B_V_WIDTH = B_HEADS * B_VAL_DIM
B_COLS = 2 * B_QK_WIDTH + 2 * B_V_WIDTH
GATE_COLS = 2 * D_MODEL
FFN_HIDDEN = 4 * D_MODEL
ROPE_BASE = 10000.0
NORM_EPS = 1e-6
PAST_LEN = 4096

LANES = 128
MXU_DIM = 256
A_PAIRS = A_WIDTH // LANES
A_CHUNK = 64
A_SUBCHUNKS = 4
B_CHUNK = 128
ROW_TILE = 512
FFN_SLAB = 1024
VMEM_LIMIT = 56 * 1024 * 1024


def _dot(a, b):
    return jnp.dot(a.astype(BF16), b.astype(BF16), preferred_element_type=F32)


def _dot_nt(a, b):
    return lax.dot_general(a.astype(BF16), b.astype(BF16), (((1,), (1,)), ((), ())),
                           preferred_element_type=F32)


def _rms(x, g):
    return x * lax.rsqrt(jnp.mean(x * x, axis=-1, keepdims=True) + NORM_EPS) * g


def _params(*sem):
    return pltpu.CompilerParams(dimension_semantics=sem, vmem_limit_bytes=VMEM_LIMIT)


def _ada_kernel(c_ref, w_ref, b_ref, o_ref):
    c = c_ref[...]
    o_ref[...] = _dot(c * jax.nn.sigmoid(c), w_ref[...]) + b_ref[...]


def _ada(c, w, b):
    n, d = c.shape
    cols = w.shape[1]
    tn = cols // 4
    return pl.pallas_call(
        _ada_kernel,
        grid=(cols // tn,),
        in_specs=[pl.BlockSpec((n, d), lambda j: (0, 0)),
                  pl.BlockSpec((d, tn), lambda j: (0, j)),
                  pl.BlockSpec((1, tn), lambda j: (0, j))],
        out_specs=pl.BlockSpec((n, tn), lambda j: (0, j)),
        out_shape=jax.ShapeDtypeStruct((n, cols), F32),
        compiler_params=_params("arbitrary"),
        name="ada",
    )(c, w, b.reshape(1, cols))


def _prenorm_kernel(x_ref, g_ref, sc_ref, sh_ref, o_ref):
    h = _rms(x_ref[...], g_ref[...]) * (1.0 + sc_ref[0]) + sh_ref[0]
    o_ref[...] = h.astype(o_ref.dtype)


def _mod_spec(mod, tiles_per_group):
    _, r, d = mod.shape
    return pl.BlockSpec((1, r, d), lambda i: (i // tiles_per_group, 0, 0))


def _prenorm(x, g, sc, sh, tiles_per_group):
    n, d = x.shape
    return pl.pallas_call(
        _prenorm_kernel,
        grid=(n // ROW_TILE,),
        in_specs=[pl.BlockSpec((ROW_TILE, d), lambda i: (i, 0)),
                  pl.BlockSpec((1, d), lambda i: (0, 0)),
                  _mod_spec(sc, tiles_per_group), _mod_spec(sh, tiles_per_group)],
        out_specs=pl.BlockSpec((ROW_TILE, d), lambda i: (i, 0)),
        out_shape=jax.ShapeDtypeStruct((n, d), BF16),
        compiler_params=_params("arbitrary"),
        name="prenorm",
    )(x, g, sc, sh)


def _matmul_kernel(a_ref, w_ref, o_ref):
    o_ref[...] = jnp.dot(a_ref[...], w_ref[...], preferred_element_type=F32).astype(o_ref.dtype)


def _matmul(a, w, tn, name):
    n, k = a.shape
    cols = w.shape[1]
    tm = min(n, 1024)
    return pl.pallas_call(
        _matmul_kernel,
        grid=(cols // tn, n // tm),
        in_specs=[pl.BlockSpec((tm, k), lambda j, i: (i, 0)),
                  pl.BlockSpec((k, tn), lambda j, i: (0, j))],
        out_specs=pl.BlockSpec((tm, tn), lambda j, i: (i, j)),
        out_shape=jax.ShapeDtypeStruct((n, cols), BF16),
        compiler_params=_params("arbitrary", "arbitrary"),
        name=name,
    )(a, w)


def _split3(x):
    h1 = x.astype(BF16)
    r1 = x - h1.astype(F32)
    h2 = r1.astype(BF16)
    h3 = (r1 - h2.astype(F32)).astype(BF16)
    return h1, h2, h3


def _head_sum(x, ones_bd):
    c = x.shape[0]
    groups = A_WIDTH // MXU_DIM
    xs = jnp.concatenate([x[:, q * MXU_DIM:(q + 1) * MXU_DIM] for q in range(groups)], axis=0)
    s = jnp.dot(xs.astype(BF16), ones_bd, preferred_element_type=F32)
    return jnp.concatenate([s[q * c:(q + 1) * c] for q in range(groups)], axis=1)


def _rwkv_init(shift0_ref, sp0_ref, sp_ref, carry_ref):
    if shift0_ref is None:
        carry_ref[...] = jnp.zeros(carry_ref.shape, F32)
        sp_ref[...] = jnp.zeros(sp_ref.shape, F32)
    else:
        carry_ref[...] = shift0_ref[0]
        sp_ref[...] = sp0_ref[0]


def _rwkv_kernel(*refs, valid_rows, n_sub, zero_init, n_alias):
    shift0_ref = sp0_ref = None
    if zero_init:
        za_ref = refs[0]
        refs = refs[1:]
    else:
        za_ref, shift0_ref, sp0_ref = refs[:3]
        refs = refs[3:]
    ya_ref, sp_ref, shift_out_ref, carry_ref = refs[11 + n_alias:]

    @pl.when(pl.program_id(1) == 0)
    def _():
        _rwkv_init(shift0_ref, sp0_ref, sp_ref, carry_ref)

    _rwkv_main(za_ref, refs[:11], ya_ref, sp_ref, shift_out_ref, carry_ref, valid_rows, n_sub)


def _rwkv_main(za_ref, prm, ya_ref, sp_ref, shift_out_ref, carry_ref, valid_rows, n_sub):
    mu_ref, w0_ref, w2_ref, a0_ref, a2_ref, g2_ref, kkp_ref, kap_ref, rk_ref, lnw_ref, lnb_ref = prm
    C = A_CHUNK
    C2 = 2 * C
    R = n_sub * C

    za = za_ref[...].astype(F32)
    row = lax.broadcasted_iota(jnp.int32, (R, 1), 0)
    za_prev = jnp.where(row == 0, carry_ref[...], pltpu.roll(za, 1, axis=0))
    carry_ref[...] = za[R - 1:R, :]
    last = R - C + valid_rows - 1
    shift_out_ref[...] = za[last:last + 1, :]
    mix = za + (za_prev - za) * mu_ref[...]

    r = mix[:, 0:A_WIDTH]
    k = mix[:, A_WIDTH:2 * A_WIDTH]
    v = mix[:, 2 * A_WIDTH:3 * A_WIDTH]
    zwi = mix[:, 3 * A_WIDTH:3 * A_WIDTH + LANES]
    zg = mix[:, 3 * A_WIDTH + LANES:A_COLS]

    lw = -math.exp(-0.5) * jax.nn.sigmoid(w0_ref[...] + _dot(jnp.tanh(zwi), w2_ref[...]))
    iclr = jax.nn.sigmoid(a0_ref[...] + _dot(zwi, a2_ref[...]))
    gate = _dot(jax.nn.sigmoid(zg), g2_ref[...])

    li = lax.broadcasted_iota(jnp.int32, (LANES, LANES), 0)
    lj = lax.broadcasted_iota(jnp.int32, (LANES, LANES), 1)
    blk_diag = (li // A_HEAD_DIM) == (lj // A_HEAD_DIM)
    mi = lax.broadcasted_iota(jnp.int32, (MXU_DIM, MXU_DIM), 0)
    mj = lax.broadcasted_iota(jnp.int32, (MXU_DIM, MXU_DIM), 1)
    ones_bd = jnp.where((mi // A_HEAD_DIM) == (mj // A_HEAD_DIM), 1.0, 0.0).astype(BF16)

    kk = k * kkp_ref[...]
    kk = kk * lax.rsqrt(jnp.maximum(_head_sum(kk * kk, ones_bd), 1e-24))
    k2 = k * (1.0 + (iclr - 1.0) * kap_ref[...])
    if valid_rows < C:
        valid = row < valid_rows
        lw = jnp.where(valid, lw, 0.0)
        kk = jnp.where(valid, kk, 0.0)
        k2 = jnp.where(valid, k2, 0.0)
    b = kk * iclr

    ti = lax.broadcasted_iota(jnp.int32, (R, R), 0)
    tj = lax.broadcasted_iota(jnp.int32, (R, R), 1)
    tril_incl = jnp.where((ti >= tj) & ((ti // C) == (tj // C)), 1.0, 0.0).astype(BF16)
    cum = sum(jnp.dot(tril_incl, part, preferred_element_type=F32) for part in _split3(lw))
    e_p = jnp.exp(cum)
    e_m = jnp.exp(-cum)
    e_tots = [e_p[j * C + C - 1:j * C + C, :] for j in range(n_sub)]
    e_rem = jnp.concatenate([e_tots[j] * e_m[j * C:(j + 1) * C] for j in range(n_sub)], axis=0)
    rt = r * e_p
    at = -kk * jnp.exp(cum - lw)
    bt = b * e_m
    kt = k2 * e_m
    kh = k2 * e_rem
    bh = b * e_rem

    gi = lax.broadcasted_iota(jnp.int32, (2 * C2, C2), 0)
    gj = lax.broadcasted_iota(jnp.int32, (2 * C2, C2), 1)
    blk = gi // C
    lower = (gi % C) >= (gj % C)
    mask_g = lower & (((blk == 1) | (blk == 2)) | ((gi % C) != (gj % C)))
    qi = lax.broadcasted_iota(jnp.int32, (C2, C2), 0)
    qj = lax.broadcasted_iota(jnp.int32, (C2, C2), 1)
    blk_tl = (qi < C) & (qj < C)
    blk_br = (qi >= C) & (qj >= C)
    eye2 = jnp.where(qi == qj, 1.0, 0.0).astype(F32)
    lane = lax.broadcasted_iota(jnp.int32, (1, LANES), 1)
    m0 = lane < A_HEAD_DIM
    zeros = jnp.zeros((C, LANES), F32)
    n_double = int(math.log2(C)) - 1

    pairs = range(A_PAIRS)
    lanes = [slice(p * LANES, (p + 1) * LANES) for p in pairs]
    rows = [slice(j * C, (j + 1) * C) for j in range(n_sub)]
    streams = [(rw, ln) for rw in rows for ln in lanes]
    cat = lambda *xs: jnp.concatenate(xs, axis=0)
    only0 = lambda x: jnp.where(m0, x, 0.0)
    only1 = lambda x: jnp.where(m0, 0.0, x)
    split01 = lambda x: only0(x[0:C]) + only1(x[C:C2])
    def state_free(st):
        gs = [jnp.where(mask_g, _dot(cat(only0(at[s]), only0(rt[s]), only1(rt[s]), only1(at[s])),
                                     cat(bt[s], kt[s]).T), 0.0) for s in st]
        pws = [jnp.where(blk_tl, g[0:C2], 0.0)
               + jnp.where(blk_br, pltpu.roll(g[C2:2 * C2], C, axis=1), 0.0) for g in gs]
        tts = [eye2 + nn for nn in pws]
        pws = [_dot(pw, pw) for pw in pws]
        for _ in range(n_double - 1):
            both = [_dot(cat(pw, tt), pw) for pw, tt in zip(pws, tts)]
            pws = [bth[0:C2] for bth in both]
            tts = [tt + bth[C2:2 * C2] for tt, bth in zip(tts, both)]
        tts = [tt + _dot(tt, pw) for tt, pw in zip(tts, pws)]
        t01s = [tt[0:C] + tt[C:C2] for tt in tts]
        akv = [split01(_dot(cat(g[0:C], g[3 * C:4 * C]), cat(zeros, v[s]))) for g, s in zip(gs, st)]
        return gs, t01s, akv

    def with_state(j, st, free, sps):
        gs, t01s, akv = free
        from_state = [_dot(cat(at[s], rt[s]), sp.T) for s, sp in zip(st, sps)]
        ws = [fs[0:C] + x for fs, x in zip(from_state, akv)]
        us = [_dot(t01, cat(only0(w), only1(w))) for t01, w in zip(t01s, ws)]
        ys = [fs[C:C2] + split01(_dot(g[C:3 * C], cat(u, v[s])))
              for fs, g, s, u in zip(from_state, gs, st, us)]
        sps = [jnp.where(blk_diag, sp * e_tots[j][:, s[1]] + _dot(cat(v[s], u).T, cat(kh[s], bh[s])), 0.0)
               for s, sp, u in zip(st, sps, us)]
        return ys, sps

    chunk_streams = [streams[j * A_PAIRS:(j + 1) * A_PAIRS] for j in range(n_sub)]
    sps = [sp_ref[p] for p in pairs]
    ys = []
    free = state_free(chunk_streams[0])
    for j in range(n_sub):
        nxt = state_free(chunk_streams[j + 1]) if j + 1 < n_sub else None
        yj, sps = with_state(j, chunk_streams[j], free, sps)
        ys.append(yj)
        free = nxt
    for p in pairs:
        sp_ref[p] = sps[p]

    y = jnp.concatenate([jnp.concatenate(yj, axis=1) for yj in ys], axis=0)
    mu = _head_sum(y, ones_bd) * (1.0 / A_HEAD_DIM)
    d = y - mu
    var = _head_sum(d * d, ones_bd) * (1.0 / A_HEAD_DIM)
    yn = d * lax.rsqrt(var + A_GN_EPS) * lnw_ref[...] + lnb_ref[...]
    bonus = _head_sum(r * k2 * rk_ref[...], ones_bd) * v
    ya_ref[...] = ((yn + bonus) * gate).astype(ya_ref.dtype)


def _stacked_out(layer, depth, bsz, tail):
    spec = pl.BlockSpec((None, None) + tail, lambda b, c: (layer, b) + (0,) * len(tail))
    return spec, jax.ShapeDtypeStruct((depth, bsz) + tail, F32)


def _rwkv(za, shift0, sp0, pr, bsz, n_steps, n_sub, valid_rows, layer, depth, prev):
    R = n_sub * A_CHUNK
    zero_init = shift0 is None
    vec = lambda n: pl.BlockSpec((1, n), lambda b, c: (0, 0))
    mat = lambda r, n: pl.BlockSpec((r, n), lambda b, c: (0, 0))
    in_specs = [pl.BlockSpec((R, A_COLS), lambda b, c: (b * n_steps + c, 0))]
    args = [za]
    if not zero_init:
        in_specs += [pl.BlockSpec((1, 1, A_COLS), lambda b, c: (b, 0, 0)),
                     pl.BlockSpec((1, A_PAIRS, LANES, LANES), lambda b, c: (b, 0, 0, 0))]
        args += [shift0, sp0]
    in_specs += [vec(A_COLS), vec(A_WIDTH), mat(LANES, A_WIDTH), vec(A_WIDTH), mat(LANES, A_WIDTH),
                 mat(A_GATE_RANK, A_WIDTH), vec(A_WIDTH), vec(A_WIDTH), vec(A_WIDTH),
                 vec(A_WIDTH), vec(A_WIDTH)]
    args += [pr["a_mu"], pr["a_w0"], pr["a_w2p"], pr["a_a0"], pr["a_a2p"], pr["a_g2"],
             pr["a_kk"], pr["a_ka"], pr["a_rk"], pr["a_ln_w"], pr["a_ln_b"]]
    aliases = {}
    if prev is not None:
        aliases = {len(args): 1, len(args) + 1: 2}
        in_specs += [pl.BlockSpec(memory_space=pl.ANY)] * 2
        args += list(prev)
    sp_spec, sp_shape = _stacked_out(layer, depth, bsz, (A_PAIRS, LANES, LANES))
    sh_spec, sh_shape = _stacked_out(layer, depth, bsz, (1, A_COLS))
    ya, sp_all, shift_all = pl.pallas_call(
        functools.partial(_rwkv_kernel, valid_rows=valid_rows, n_sub=n_sub, zero_init=zero_init,
                          n_alias=len(aliases)),
        grid=(bsz, n_steps),
        in_specs=in_specs,
        out_specs=[pl.BlockSpec((R, A_WIDTH), lambda b, c: (b * n_steps + c, 0)), sp_spec, sh_spec],
        out_shape=[jax.ShapeDtypeStruct((bsz * n_steps * R, A_WIDTH), BF16), sp_shape, sh_shape],
        scratch_shapes=[pltpu.VMEM((1, A_COLS), F32)],
        input_output_aliases=aliases,
        compiler_params=_params("arbitrary", "arbitrary"),
        name="rwkv",
    )(*args)
    return ya, (sp_all, shift_all)


def _pack_pairs(s):
    bsz = s.shape[0]
    s = s.reshape(bsz, A_PAIRS, 2, A_HEAD_DIM, A_HEAD_DIM)
    z = jnp.zeros_like(s[:, :, 0])
    top = jnp.concatenate([s[:, :, 0], z], axis=-1)
    bot = jnp.concatenate([z, s[:, :, 1]], axis=-1)
    return jnp.concatenate([top, bot], axis=-2)


def _unpack_pairs(sp):
    h = A_HEAD_DIM
    s = jnp.stack([sp[..., :h, :h], sp[..., h:, h:]], axis=-3)
    return s.reshape(sp.shape[:-3] + (A_HEADS, h, h))


def _ret_kernel(*refs, valid_rows, zero_init, n_alias):
    s0_ref = None
    if zero_init:
        zb_ref, cos_ref, sin_ref = refs[:3]
    else:
        zb_ref, cos_ref, sin_ref, s0_ref = refs[:4]
    yb_ref, s_ref = refs[-2:]

    @pl.when(pl.program_id(1) == 0)
    def _():
        _ret_init(s0_ref, s_ref)

    _ret_main(zb_ref, cos_ref, sin_ref, yb_ref, s_ref, valid_rows, 1)


def _ret_init(s0_ref, s_ref):
    if s0_ref is None:
        s_ref[...] = jnp.zeros(s_ref.shape, F32)
    else:
        s_ref[...] = s0_ref[0]


def _ret_main(zb_ref, cos_ref, sin_ref, yb_ref, s_ref, valid_rows, n_chunks):
    C = B_CHUNK
    half = B_KEY_DIM // 2
    ri = lax.broadcasted_iota(jnp.int32, (C, C), 0)
    rj = lax.broadcasted_iota(jnp.int32, (C, C), 1)
    diff = (ri - rj).astype(F32)
    idx = lax.broadcasted_iota(jnp.int32, (C, 1), 0).astype(F32)

    for i in range(n_chunks):
        rs = slice(i * C, (i + 1) * C)
        cos = cos_ref[rs, :]
        sin = sin_ref[rs, :]

        def rope(x):
            x1, x2 = x[:, :half], x[:, half:]
            return jnp.concatenate([x1 * cos - x2 * sin, x1 * sin + x2 * cos], axis=1)

        for h in range(B_HEADS):
            log_g = math.log(1.0 - 2.0 ** (-5.0 - h))
            q = rope(zb_ref[rs, h * B_KEY_DIM:(h + 1) * B_KEY_DIM].astype(F32))
            k = rope(zb_ref[rs, B_QK_WIDTH + h * B_KEY_DIM:B_QK_WIDTH + (h + 1) * B_KEY_DIM].astype(F32))
            k = k * (B_KEY_DIM ** -0.5)
            v = zb_ref[rs, 2 * B_QK_WIDTH + h * B_VAL_DIM:2 * B_QK_WIDTH + (h + 1) * B_VAL_DIM]
            g = zb_ref[rs, 2 * B_QK_WIDTH + B_V_WIDTH + h * B_VAL_DIM:
                       2 * B_QK_WIDTH + B_V_WIDTH + (h + 1) * B_VAL_DIM].astype(F32)
            mask = jnp.where(diff >= 0.0, jnp.exp(jnp.maximum(diff, 0.0) * log_g), 0.0)
            scores = _dot_nt(q, k) * mask
            s = s_ref[h]
            y = _dot(scores, v) + _dot(q * jnp.exp((idx + 1.0) * log_g), s)
            k_dec = jnp.where(idx < valid_rows, k * jnp.exp((valid_rows - 1.0 - idx) * log_g), 0.0)
            s_ref[h] = s * math.exp(valid_rows * log_g) + _dot(k_dec.T, v)
            yn = y * lax.rsqrt(jnp.mean(y * y, axis=-1, keepdims=True) + NORM_EPS)
            yb_ref[rs, h * B_VAL_DIM:(h + 1) * B_VAL_DIM] = (g * jax.nn.sigmoid(g) * yn).astype(yb_ref.dtype)


def _ret(zb, cos, sin, s0, bsz, n_chunks, valid_rows, layer, depth, prev):
    C = B_CHUNK
    half = B_KEY_DIM // 2
    zero_init = s0 is None
    in_specs = [pl.BlockSpec((C, B_COLS), lambda b, c: (b * n_chunks + c, 0)),
                pl.BlockSpec((C, half), lambda b, c: (c, 0)),
                pl.BlockSpec((C, half), lambda b, c: (c, 0))]
    args = [zb, cos, sin]
    if not zero_init:
        in_specs.append(pl.BlockSpec((None, 1, B_HEADS, B_KEY_DIM, B_VAL_DIM),
                                     lambda b, c: (layer, b, 0, 0, 0)))
        args.append(s0)
    aliases = {}
    if prev is not None:
        aliases = {len(args): 1}
        in_specs.append(pl.BlockSpec(memory_space=pl.ANY))
        args.append(prev)
    s_spec, s_shape = _stacked_out(layer, depth, bsz, (B_HEADS, B_KEY_DIM, B_VAL_DIM))
    return pl.pallas_call(
        functools.partial(_ret_kernel, valid_rows=valid_rows, zero_init=zero_init, n_alias=len(aliases)),
        grid=(bsz, n_chunks),
        in_specs=in_specs,
        out_specs=[pl.BlockSpec((C, B_V_WIDTH), lambda b, c: (b * n_chunks + c, 0)), s_spec],
        out_shape=[jax.ShapeDtypeStruct((bsz * n_chunks * C, B_V_WIDTH), BF16), s_shape],
        input_output_aliases=aliases,
        compiler_params=_params("arbitrary", "arbitrary"),
        name="ret",
    )(*args)


def _mixers_kernel(*refs, n_sub, n_ret, n_alias):
    za_ref, zb_ref, cos_ref, sin_ref = refs[:4]
    prm = refs[4:15]
    ya_ref, sp_ref, shift_out_ref, yb_ref, s_ref, carry_ref = refs[15 + n_alias:]

    @pl.when(pl.program_id(1) == 0)
    def _():
        _rwkv_init(None, None, sp_ref, carry_ref)
        _ret_init(None, s_ref)

    _rwkv_main(za_ref, prm, ya_ref, sp_ref, shift_out_ref, carry_ref, A_CHUNK, n_sub)
    _ret_main(zb_ref, cos_ref, sin_ref, yb_ref, s_ref, B_CHUNK, n_ret)


def _mixers(za, zb, cos, sin, pr, bsz, n_steps, layer, depth, prev):
    R = A_SUBCHUNKS * A_CHUNK
    half = B_KEY_DIM // 2
    rows = lambda w: pl.BlockSpec((R, w), lambda b, c: (b * n_steps + c, 0))
    vec = lambda n: pl.BlockSpec((1, n), lambda b, c: (0, 0))
    mat = lambda r, n: pl.BlockSpec((r, n), lambda b, c: (0, 0))
    in_specs = [rows(A_COLS), rows(B_COLS),
                pl.BlockSpec((R, half), lambda b, c: (c, 0)), pl.BlockSpec((R, half), lambda b, c: (c, 0)),
                vec(A_COLS), vec(A_WIDTH), mat(LANES, A_WIDTH), vec(A_WIDTH), mat(LANES, A_WIDTH),
                mat(A_GATE_RANK, A_WIDTH), vec(A_WIDTH), vec(A_WIDTH), vec(A_WIDTH),
                vec(A_WIDTH), vec(A_WIDTH)]
    args = [za, zb, cos, sin, pr["a_mu"], pr["a_w0"], pr["a_w2p"], pr["a_a0"], pr["a_a2p"], pr["a_g2"],
            pr["a_kk"], pr["a_ka"], pr["a_rk"], pr["a_ln_w"], pr["a_ln_b"]]
    aliases = {}
    if prev is not None:
        (sp_prev, shift_prev), ret_prev = prev
        aliases = {len(args): 1, len(args) + 1: 2, len(args) + 2: 4}
        in_specs += [pl.BlockSpec(memory_space=pl.ANY)] * 3
        args += [sp_prev, shift_prev, ret_prev]
    sp_spec, sp_shape = _stacked_out(layer, depth, bsz, (A_PAIRS, LANES, LANES))
    sh_spec, sh_shape = _stacked_out(layer, depth, bsz, (1, A_COLS))
    s_spec, s_shape = _stacked_out(layer, depth, bsz, (B_HEADS, B_KEY_DIM, B_VAL_DIM))
    ya, sp_all, shift_all, yb, ret_all = pl.pallas_call(
        functools.partial(_mixers_kernel, n_sub=A_SUBCHUNKS, n_ret=R // B_CHUNK, n_alias=len(aliases)),
        grid=(bsz, n_steps),
        in_specs=in_specs,
        out_specs=[rows(A_WIDTH), sp_spec, sh_spec, rows(B_V_WIDTH), s_spec],
        out_shape=[jax.ShapeDtypeStruct((bsz * n_steps * R, A_WIDTH), BF16), sp_shape, sh_shape,
                   jax.ShapeDtypeStruct((bsz * n_steps * R, B_V_WIDTH), BF16), s_shape],
        scratch_shapes=[pltpu.VMEM((1, A_COLS), F32)],
        input_output_aliases=aliases,
        compiler_params=_params("arbitrary", "arbitrary"),
        name="mixers",
    )(*args)
    return ya, yb, ((sp_all, shift_all), ret_all)


def _merge_kernel(ya_ref, yb_ref, zg_ref, x_ref, gt_ref, g_ref, wa_ref, wb_ref, wo_ref, o_ref):
    pa = jnp.dot(ya_ref[...], wa_ref[...], preferred_element_type=F32)
    pb = jnp.dot(yb_ref[...], wb_ref[...], preferred_element_type=F32)
    ga = jax.nn.sigmoid(zg_ref[:, :D_MODEL].astype(F32))
    gb = jax.nn.sigmoid(zg_ref[:, D_MODEL:].astype(F32))
    out = _dot(ga * pa + gb * pb, wo_ref[...])
    o_ref[...] = x_ref[...] + gt_ref[0] * _rms(out, g_ref[...])


def _resident(shape):
    return pl.BlockSpec(shape, lambda i: (0,) * len(shape), pipeline_mode=pl.Buffered(1))


def _merge(ya, yb, zg, x, gt, g, wa, wb, wo, tiles_per_group):
    n, d = x.shape
    rows = lambda w: pl.BlockSpec((ROW_TILE, w), lambda i: (i, 0))
    return pl.pallas_call(
        _merge_kernel,
        grid=(n // ROW_TILE,),
        in_specs=[rows(A_WIDTH), rows(B_V_WIDTH), rows(GATE_COLS), rows(d),
                  _mod_spec(gt, tiles_per_group), _resident((1, d)),
                  _resident(wa.shape), _resident(wb.shape), _resident(wo.shape)],
        out_specs=rows(d),
        out_shape=jax.ShapeDtypeStruct((n, d), F32),
        compiler_params=_params("arbitrary"),
        name="merge",
    )(ya, yb, zg, x, gt, g, wa, wb, wo)


def _ffn_kernel(x_ref, sc_ref, sh_ref, gt_ref, gpre_ref, gpost_ref, up_ref, down_ref, o_ref):
    x = x_ref[...]
    h = (_rms(x, gpre_ref[...]) * (1.0 + sc_ref[0]) + sh_ref[0]).astype(BF16)
    f = jnp.zeros(x.shape, F32)
    for j in range(FFN_HIDDEN // FFN_SLAB):
        sl = slice(j * FFN_SLAB, (j + 1) * FFN_SLAB)
        a = jnp.maximum(jnp.dot(h, up_ref[:, sl], preferred_element_type=F32), 0.0)
        f = f + _dot(a * a, down_ref[sl, :])
    o_ref[...] = x + gt_ref[0] * _rms(f, gpost_ref[...])


def _ffn(x, sc, sh, gt, gpre, gpost, up, down, tiles_per_group):
    n, d = x.shape
    rows = pl.BlockSpec((ROW_TILE, d), lambda i: (i, 0))
    mod = _mod_spec(sc, tiles_per_group)
    return pl.pallas_call(
        _ffn_kernel,
        grid=(n // ROW_TILE,),
        in_specs=[rows, mod, mod, mod, _resident((1, d)), _resident((1, d)),
                  _resident(up.shape), _resident(down.shape)],
        out_specs=rows,
        out_shape=jax.ShapeDtypeStruct((n, d), F32),
        compiler_params=_params("arbitrary"),
        name="ffn",
    )(x, sc, sh, gt, gpre, gpost, up, down)


def _pad_frames(z, bsz, t, t_pad):
    if t_pad == t:
        return z
    z = z.reshape(bsz, t, -1)
    return jnp.pad(z, ((0, 0), (0, t_pad - t), (0, 0))).reshape(bsz * t_pad, -1)


def _drop_frames(y, bsz, t, t_pad):
    if t_pad == t:
        return y
    return y.reshape(bsz, t_pad, -1)[:, :t].reshape(bsz * t, -1)


def _layer(x, mod, shift0, sp0, ret0, pos0, pr, bsz, t, layer, depth, prev):
    n = bsz * t
    if t % ROW_TILE == 0:
        tiles_per_group = t // ROW_TILE
        mods = [m.reshape(bsz, 1, D_MODEL) for m in jnp.split(mod, 6, axis=-1)]
    else:
        tiles_per_group = 1
        mods = [jnp.repeat(m, t, axis=0).reshape(n // ROW_TILE, ROW_TILE, D_MODEL)
                for m in jnp.split(mod, 6, axis=-1)]
    sh1, sc1, gt1, sh2, sc2, gt2 = mods

    h = _prenorm(x, pr["norm_pre0"], sc1, sh1, tiles_per_group)
    za = _matmul(h, pr["w_in_a"], A_COLS, "in_proj_a")
    zb = _matmul(h, pr["w_in_b"], B_COLS // 4, "in_proj_b")
    zg = _matmul(h, pr["w_in_g"], GATE_COLS, "in_proj_g")

    half = B_KEY_DIM // 2
    inv_freq = ROPE_BASE ** (-jnp.linspace(0.0, 1.0, half, dtype=F32))
    ra = A_SUBCHUNKS * A_CHUNK
    if shift0 is None and t % ra == 0:
        ang = (pos0 + jnp.arange(t, dtype=jnp.int32)).astype(F32)[:, None] * inv_freq[None, :]
        ya, yb, states = _mixers(za, zb, jnp.cos(ang), jnp.sin(ang), pr, bsz, t // ra, layer, depth, prev)
    else:
        ta = -(-t // A_CHUNK) * A_CHUNK
        ya, rwkv_states = _rwkv(_pad_frames(za, bsz, t, ta), shift0, sp0, pr, bsz, ta // A_CHUNK, 1,
                                min(t, A_CHUNK), layer, depth, None if prev is None else prev[0])
        ya = _drop_frames(ya, bsz, t, ta)
        tb = -(-t // B_CHUNK) * B_CHUNK
        ang = (pos0 + jnp.arange(tb, dtype=jnp.int32)).astype(F32)[:, None] * inv_freq[None, :]
        yb, ret_states = _ret(_pad_frames(zb, bsz, t, tb), jnp.cos(ang), jnp.sin(ang), ret0, bsz,
                              tb // B_CHUNK, min(t, B_CHUNK), layer, depth, None if prev is None else prev[1])
        yb = _drop_frames(yb, bsz, t, tb)
        states = (rwkv_states, ret_states)

    x = _merge(ya, yb, zg, x, gt1, pr["norm_post0"], pr["w_branch_a"], pr["w_branch_b"], pr["w_out"],
               tiles_per_group)
    x = _ffn(x, sc2, sh2, gt2, pr["norm_pre1"], pr["norm_post1"], pr["w_ffn_up"], pr["w_ffn_down"],
             tiles_per_group)
    return x, states


def kernel(x_prompt, x_sample, state_shift, state_rwkv, state_ret, c_prompt, c_sample, ada_w, ada_b, norm_pre, norm_post, w_in, a_mu, a_w0, a_w2, a_a0, a_a2, a_g2, a_kk, a_ka, a_rk, a_ln_w, a_ln_b, w_branch_a, w_branch_b, w_out, w_ffn_up, w_ffn_down):
    depth = w_in.shape[0]
    bp, tp, _ = x_prompt.shape
    bs, ts, _ = x_sample.shape
    xp = x_prompt.reshape(bp * tp, D_MODEL)
    xs = x_sample.reshape(bs * ts, D_MODEL)
    c_all = jnp.concatenate([c_prompt, c_sample], axis=0)
    row = lambda a: a.reshape(1, -1)
    lora_pad = jnp.zeros((A_DECAY_RANK, A_WIDTH), BF16)

    prev_p = prev_s = None
    for l in range(depth):
        wl = w_in[l].astype(BF16)
        pr = {
            "norm_pre0": row(norm_pre[l, 0]), "norm_pre1": row(norm_pre[l, 1]),
            "norm_post0": row(norm_post[l, 0]), "norm_post1": row(norm_post[l, 1]),
            "w_in_a": wl[:, :A_COLS], "w_in_b": wl[:, A_COLS:A_COLS + B_COLS],
            "w_in_g": wl[:, A_COLS + B_COLS:],
            "a_mu": row(a_mu[l]), "a_w0": row(a_w0[l]), "a_a0": row(a_a0[l]),
            "a_w2p": jnp.concatenate([a_w2[l].astype(BF16), lora_pad], axis=0),
            "a_a2p": jnp.concatenate([lora_pad, a_a2[l].astype(BF16)], axis=0),
            "a_g2": a_g2[l].astype(BF16),
            "a_kk": row(a_kk[l]), "a_ka": row(a_ka[l]), "a_rk": row(a_rk[l]),
            "a_ln_w": row(a_ln_w[l]), "a_ln_b": row(a_ln_b[l]),
            "w_branch_a": w_branch_a[l].astype(BF16), "w_branch_b": w_branch_b[l].astype(BF16),
            "w_out": w_out[l].astype(BF16),
            "w_ffn_up": w_ffn_up[l].astype(BF16), "w_ffn_down": w_ffn_down[l].astype(BF16),
        }
        mod = _ada(c_all, ada_w[l].astype(BF16), ada_b[l])
        xp, prev_p = _layer(xp, mod[:bp], None, None, None, 0, pr, bp, tp, l, depth, prev_p)
        xs, prev_s = _layer(xs, mod[bp:], state_shift[l].reshape(bs, 1, A_COLS),
                            _pack_pairs(state_rwkv[l]), state_ret, PAST_LEN, pr, bs, ts, l, depth, prev_s)

    def states(prev, bsz):
        (sp_all, shift_all), ret_all = prev
        return shift_all.reshape(depth, bsz, A_COLS), _unpack_pairs(sp_all), ret_all

    return (xp.reshape(bp, tp, D_MODEL), xs.reshape(bs, ts, D_MODEL), *states(prev_p, bp), *states(prev_s, bs))
```

```python
import functools
import math

import jax
import jax.numpy as jnp
from jax import lax
from jax.experimental import pallas as pl
from jax.experimental.pallas import tpu as pltpu

F32 = jnp.float32
BF16 = jnp.bfloat16

D_MODEL = 1024
A_HEADS = 16
A_HEAD_DIM = 64
A_WIDTH = A_HEADS * A_HEAD_DIM
A_DECAY_RANK = 64
A_ICLR_RANK = 64
A_GATE_RANK = 128
A_COLS = 3 * A_WIDTH + A_DECAY_RANK + A_ICLR_RANK + A_GATE_RANK
A_GN_EPS = 64e-5
B_HEADS = 4
B_KEY_DIM = 256
B_VAL_DIM = 512
B_QK_WIDTH = B_HEADS * B_KEY_DIM
B_V_WIDTH = B_HEADS * B_VAL_DIM
B_COLS = 2 * B_QK_WIDTH + 2 * B_V_WIDTH
GATE_COLS = 2 * D_MODEL
FFN_HIDDEN = 4 * D_MODEL
ROPE_BASE = 10000.0
NORM_EPS = 1e-6
PAST_LEN = 4096

LANES = 128
MXU_DIM = 256
A_PAIRS = A_WIDTH // LANES
A_CHUNK = 64
A_SUBCHUNKS = 4
B_CHUNK = 128
ROW_TILE = 512
FFN_SLAB = 1024
VMEM_LIMIT = 56 * 1024 * 1024


def _dot(a, b):
    return jnp.dot(a.astype(BF16), b.astype(BF16), preferred_element_type=F32)


def _dot_nt(a, b):
    return lax.dot_general(a.astype(BF16), b.astype(BF16), (((1,), (1,)), ((), ())),
                           preferred_element_type=F32)


def _rms(x, g):
    return x * lax.rsqrt(jnp.mean(x * x, axis=-1, keepdims=True) + NORM_EPS) * g


def _params(*sem):
    return pltpu.CompilerParams(dimension_semantics=sem, vmem_limit_bytes=VMEM_LIMIT)


def _ada_kernel(c_ref, w_ref, b_ref, o_ref):
    c = c_ref[...]
    o_ref[...] = _dot(c * jax.nn.sigmoid(c), w_ref[...]) + b_ref[...]


def _ada(c, w, b):
    n, d = c.shape
    cols = w.shape[1]
    tn = cols // 4
    return pl.pallas_call(
        _ada_kernel,
        grid=(cols // tn,),
        in_specs=[pl.BlockSpec((n, d), lambda j: (0, 0)),
                  pl.BlockSpec((d, tn), lambda j: (0, j)),
                  pl.BlockSpec((1, tn), lambda j: (0, j))],
        out_specs=pl.BlockSpec((n, tn), lambda j: (0, j)),
        out_shape=jax.ShapeDtypeStruct((n, cols), F32),
        compiler_params=_params("arbitrary"),
        name="ada",
    )(c, w, b.reshape(1, cols))


def _prenorm_kernel(x_ref, g_ref, sc_ref, sh_ref, o_ref):
    h = _rms(x_ref[...], g_ref[...]) * (1.0 + sc_ref[0]) + sh_ref[0]
    o_ref[...] = h.astype(o_ref.dtype)


def _mod_spec(mod, tiles_per_group):
    _, r, d = mod.shape
    return pl.BlockSpec((1, r, d), lambda i: (i // tiles_per_group, 0, 0))


def _prenorm(x, g, sc, sh, tiles_per_group):
    n, d = x.shape
    return pl.pallas_call(
        _prenorm_kernel,
        grid=(n // ROW_TILE,),
        in_specs=[pl.BlockSpec((ROW_TILE, d), lambda i: (i, 0)),
                  pl.BlockSpec((1, d), lambda i: (0, 0)),
                  _mod_spec(sc, tiles_per_group), _mod_spec(sh, tiles_per_group)],
        out_specs=pl.BlockSpec((ROW_TILE, d), lambda i: (i, 0)),
        out_shape=jax.ShapeDtypeStruct((n, d), BF16),
        compiler_params=_params("arbitrary"),
        name="prenorm",
    )(x, g, sc, sh)


def _matmul_kernel(a_ref, w_ref, o_ref):
    o_ref[...] = jnp.dot(a_ref[...], w_ref[...], preferred_element_type=F32).astype(o_ref.dtype)


def _matmul(a, w, tn, name):
    n, k = a.shape
    cols = w.shape[1]
    tm = min(n, 1024)
    return pl.pallas_call(
        _matmul_kernel,
        grid=(cols // tn, n // tm),
        in_specs=[pl.BlockSpec((tm, k), lambda j, i: (i, 0)),
                  pl.BlockSpec((k, tn), lambda j, i: (0, j))],
        out_specs=pl.BlockSpec((tm, tn), lambda j, i: (i, j)),
        out_shape=jax.ShapeDtypeStruct((n, cols), BF16),
        compiler_params=_params("arbitrary", "arbitrary"),
        name=name,
    )(a, w)


def _split3(x):
    h1 = x.astype(BF16)
    r1 = x - h1.astype(F32)
    h2 = r1.astype(BF16)
    h3 = (r1 - h2.astype(F32)).astype(BF16)
    return h1, h2, h3


def _head_sum(x, ones_bd):
    c = x.shape[0]
    groups = A_WIDTH // MXU_DIM
    xs = jnp.concatenate([x[:, q * MXU_DIM:(q + 1) * MXU_DIM] for q in range(groups)], axis=0)
    s = jnp.dot(xs.astype(BF16), ones_bd, preferred_element_type=F32)
    return jnp.concatenate([s[q * c:(q + 1) * c] for q in range(groups)], axis=1)


def _rwkv_init(shift0_ref, sp0_ref, sp_ref, carry_ref):
    if shift0_ref is None:
        carry_ref[...] = jnp.zeros(carry_ref.shape, F32)
        sp_ref[...] = jnp.zeros(sp_ref.shape, F32)
    else:
        carry_ref[...] = shift0_ref[0]
        sp_ref[...] = sp0_ref[0]


def _rwkv_kernel(*refs, valid_rows, n_sub, zero_init, n_alias):
    shift0_ref = sp0_ref = None
    if zero_init:
        za_ref = refs[0]
        refs = refs[1:]
    else:
        za_ref, shift0_ref, sp0_ref = refs[:3]
        refs = refs[3:]
    ya_ref, sp_ref, shift_out_ref, carry_ref = refs[11 + n_alias:]

    @pl.when(pl.program_id(1) == 0)
    def _():
        _rwkv_init(shift0_ref, sp0_ref, sp_ref, carry_ref)

    _rwkv_main(za_ref, refs[:11], ya_ref, sp_ref, shift_out_ref, carry_ref, valid_rows, n_sub)


def _rwkv_main(za_ref, prm, ya_ref, sp_ref, shift_out_ref, carry_ref, valid_rows, n_sub):
    mu_ref, w0_ref, w2_ref, a0_ref, a2_ref, g2_ref, kkp_ref, kap_ref, rk_ref, lnw_ref, lnb_ref = prm
    C = A_CHUNK
    C2 = 2 * C
    R = n_sub * C

    za = za_ref[...].astype(F32)
    row = lax.broadcasted_iota(jnp.int32, (R, 1), 0)
    za_prev = jnp.where(row == 0, carry_ref[...], pltpu.roll(za, 1, axis=0))
    carry_ref[...] = za[R - 1:R, :]
    last = R - C + valid_rows - 1
    shift_out_ref[...] = za[last:last + 1, :]
    mix = za + (za_prev - za) * mu_ref[...]

    r = mix[:, 0:A_WIDTH]
    k = mix[:, A_WIDTH:2 * A_WIDTH]
    v = mix[:, 2 * A_WIDTH:3 * A_WIDTH]
    zwi = mix[:, 3 * A_WIDTH:3 * A_WIDTH + LANES]
    zg = mix[:, 3 * A_WIDTH + LANES:A_COLS]

    lw = -math.exp(-0.5) * jax.nn.sigmoid(w0_ref[...] + _dot(jnp.tanh(zwi), w2_ref[...]))
    iclr = jax.nn.sigmoid(a0_ref[...] + _dot(zwi, a2_ref[...]))
    gate = _dot(jax.nn.sigmoid(zg), g2_ref[...])

    li = lax.broadcasted_iota(jnp.int32, (LANES, LANES), 0)
    lj = lax.broadcasted_iota(jnp.int32, (LANES, LANES), 1)
    blk_diag = (li // A_HEAD_DIM) == (lj // A_HEAD_DIM)
    mi = lax.broadcasted_iota(jnp.int32, (MXU_DIM, MXU_DIM), 0)
    mj = lax.broadcasted_iota(jnp.int32, (MXU_DIM, MXU_DIM), 1)
    ones_bd = jnp.where((mi // A_HEAD_DIM) == (mj // A_HEAD_DIM), 1.0, 0.0).astype(BF16)

    kk = k * kkp_ref[...]
    kk = kk * lax.rsqrt(jnp.maximum(_head_sum(kk * kk, ones_bd), 1e-24))
    k2 = k * (1.0 + (iclr - 1.0) * kap_ref[...])
    if valid_rows < C:
        valid = row < valid_rows
        lw = jnp.where(valid, lw, 0.0)
        kk = jnp.where(valid, kk, 0.0)
        k2 = jnp.where(valid, k2, 0.0)
    b = kk * iclr

    ti = lax.broadcasted_iota(jnp.int32, (R, R), 0)
    tj = lax.broadcasted_iota(jnp.int32, (R, R), 1)
    tril_incl = jnp.where((ti >= tj) & ((ti // C) == (tj // C)), 1.0, 0.0).astype(BF16)
    cum = sum(jnp.dot(tril_incl, part, preferred_element_type=F32) for part in _split3(lw))
    e_p = jnp.exp(cum)
    e_m = jnp.exp(-cum)
    e_tots = [e_p[j * C + C - 1:j * C + C, :] for j in range(n_sub)]
    e_rem = jnp.concatenate([e_tots[j] * e_m[j * C:(j + 1) * C] for j in range(n_sub)], axis=0)
    rt = r * e_p
    at = -kk * jnp.exp(cum - lw)
    bt = b * e_m
    kt = k2 * e_m
    kh = k2 * e_rem
    bh = b * e_rem

    gi = lax.broadcasted_iota(jnp.int32, (2 * C2, C2), 0)
    gj = lax.broadcasted_iota(jnp.int32, (2 * C2, C2), 1)
    blk = gi // C
    lower = (gi % C) >= (gj % C)
    mask_g = lower & (((blk == 1) | (blk == 2)) | ((gi % C) != (gj % C)))
    qi = lax.broadcasted_iota(jnp.int32, (C2, C2), 0)
    qj = lax.broadcasted_iota(jnp.int32, (C2, C2), 1)
    blk_tl = (qi < C) & (qj < C)
    blk_br = (qi >= C) & (qj >= C)
    eye2 = jnp.where(qi == qj, 1.0, 0.0).astype(F32)
    lane = lax.broadcasted_iota(jnp.int32, (1, LANES), 1)
    m0 = lane < A_HEAD_DIM
    zeros = jnp.zeros((C, LANES), F32)
    n_double = int(math.log2(C)) - 1

    pairs = range(A_PAIRS)
    lanes = [slice(p * LANES, (p + 1) * LANES) for p in pairs]
    rows = [slice(j * C, (j + 1) * C) for j in range(n_sub)]
    streams = [(rw, ln) for rw in rows for ln in lanes]
    cat = lambda *xs: jnp.concatenate(xs, axis=0)
    only0 = lambda x: jnp.where(m0, x, 0.0)
    only1 = lambda x: jnp.where(m0, 0.0, x)
    split01 = lambda x: only0(x[0:C]) + only1(x[C:C2])
    def state_free(st):
        gs = [jnp.where(mask_g, _dot(cat(only0(at[s]), only0(rt[s]), only1(rt[s]), only1(at[s])),
                                     cat(bt[s], kt[s]).T), 0.0) for s in st]
        pws = [jnp.where(blk_tl, g[0:C2], 0.0)
               + jnp.where(blk_br, pltpu.roll(g[C2:2 * C2], C, axis=1), 0.0) for g in gs]
        tts = [eye2 + nn for nn in pws]
        pws = [_dot(pw, pw) for pw in pws]
        for _ in range(n_double - 1):
            both = [_dot(cat(pw, tt), pw) for pw, tt in zip(pws, tts)]
            pws = [bth[0:C2] for bth in both]
            tts = [tt + bth[C2:2 * C2] for tt, bth in zip(tts, both)]
        tts = [tt + _dot(tt, pw) for tt, pw in zip(tts, pws)]
        t01s = [tt[0:C] + tt[C:C2] for tt in tts]
        akv = [split01(_dot(cat(g[0:C], g[3 * C:4 * C]), cat(zeros, v[s]))) for g, s in zip(gs, st)]
        return gs, t01s, akv

    def with_state(j, st, free, sps):
        gs, t01s, akv = free
        from_state = [_dot(cat(at[s], rt[s]), sp.T) for s, sp in zip(st, sps)]
        ws = [fs[0:C] + x for fs, x in zip(from_state, akv)]
        us = [_dot(t01, cat(only0(w), only1(w))) for t01, w in zip(t01s, ws)]
        ys = [fs[C:C2] + split01(_dot(g[C:3 * C], cat(u, v[s])))
              for fs, g, s, u in zip(from_state, gs, st, us)]
        sps = [jnp.where(blk_diag, sp * e_tots[j][:, s[1]] + _dot(cat(v[s], u).T, cat(kh[s], bh[s])), 0.0)
               for s, sp, u in zip(st, sps, us)]
        return ys, sps

    chunk_streams = [streams[j * A_PAIRS:(j + 1) * A_PAIRS] for j in range(n_sub)]
    sps = [sp_ref[p] for p in pairs]
    ys = []
    free = state_free(chunk_streams[0])
    for j in range(n_sub):
        nxt = state_free(chunk_streams[j + 1]) if j + 1 < n_sub else None
        yj, sps = with_state(j, chunk_streams[j], free, sps)
        ys.append(yj)
        free = nxt
    for p in pairs:
        sp_ref[p] = sps[p]

    y = jnp.concatenate([jnp.concatenate(yj, axis=1) for yj in ys], axis=0)
    mu = _head_sum(y, ones_bd) * (1.0 / A_HEAD_DIM)
    d = y - mu
    var = _head_sum(d * d, ones_bd) * (1.0 / A_HEAD_DIM)
    yn = d * lax.rsqrt(var + A_GN_EPS) * lnw_ref[...] + lnb_ref[...]
    bonus = _head_sum(r * k2 * rk_ref[...], ones_bd) * v
    ya_ref[...] = ((yn + bonus) * gate).astype(ya_ref.dtype)


def _stacked_out(layer, depth, bsz, tail):
    spec = pl.BlockSpec((None, None) + tail, lambda b, c: (layer, b) + (0,) * len(tail))
    return spec, jax.ShapeDtypeStruct((depth, bsz) + tail, F32)


def _rwkv(za, shift0, sp0, pr, bsz, n_steps, n_sub, valid_rows, layer, depth, prev):
    R = n_sub * A_CHUNK
    zero_init = shift0 is None
    vec = lambda n: pl.BlockSpec((1, n), lambda b, c: (0, 0))
    mat = lambda r, n: pl.BlockSpec((r, n), lambda b, c: (0, 0))
    in_specs = [pl.BlockSpec((R, A_COLS), lambda b, c: (b * n_steps + c, 0))]
    args = [za]
    if not zero_init:
        in_specs += [pl.BlockSpec((1, 1, A_COLS), lambda b, c: (b, 0, 0)),
                     pl.BlockSpec((1, A_PAIRS, LANES, LANES), lambda b, c: (b, 0, 0, 0))]
        args += [shift0, sp0]
    in_specs += [vec(A_COLS), vec(A_WIDTH), mat(LANES, A_WIDTH), vec(A_WIDTH), mat(LANES, A_WIDTH),
                 mat(A_GATE_RANK, A_WIDTH), vec(A_WIDTH), vec(A_WIDTH), vec(A_WIDTH),
                 vec(A_WIDTH), vec(A_WIDTH)]
    args += [pr["a_mu"], pr["a_w0"], pr["a_w2p"], pr["a_a0"], pr["a_a2p"], pr["a_g2"],
             pr["a_kk"], pr["a_ka"], pr["a_rk"], pr["a_ln_w"], pr["a_ln_b"]]
    aliases = {}
    if prev is not None:
        aliases = {len(args): 1, len(args) + 1: 2}
        in_specs += [pl.BlockSpec(memory_space=pl.ANY)] * 2
        args += list(prev)
    sp_spec, sp_shape = _stacked_out(layer, depth, bsz, (A_PAIRS, LANES, LANES))
    sh_spec, sh_shape = _stacked_out(layer, depth, bsz, (1, A_COLS))
    ya, sp_all, shift_all = pl.pallas_call(
        functools.partial(_rwkv_kernel, valid_rows=valid_rows, n_sub=n_sub, zero_init=zero_init,
                          n_alias=len(aliases)),
        grid=(bsz, n_steps),
        in_specs=in_specs,
        out_specs=[pl.BlockSpec((R, A_WIDTH), lambda b, c: (b * n_steps + c, 0)), sp_spec, sh_spec],
        out_shape=[jax.ShapeDtypeStruct((bsz * n_steps * R, A_WIDTH), BF16), sp_shape, sh_shape],
        scratch_shapes=[pltpu.VMEM((1, A_COLS), F32)],
        input_output_aliases=aliases,
        compiler_params=_params("arbitrary", "arbitrary"),
        name="rwkv",
    )(*args)
    return ya, (sp_all, shift_all)


def _pack_pairs(s):
    bsz = s.shape[0]
    s = s.reshape(bsz, A_PAIRS, 2, A_HEAD_DIM, A_HEAD_DIM)
    z = jnp.zeros_like(s[:, :, 0])
    top = jnp.concatenate([s[:, :, 0], z], axis=-1)
    bot = jnp.concatenate([z, s[:, :, 1]], axis=-1)
    return jnp.concatenate([top, bot], axis=-2)


def _unpack_pairs(sp):
    h = A_HEAD_DIM
    s = jnp.stack([sp[..., :h, :h], sp[..., h:, h:]], axis=-3)
    return s.reshape(sp.shape[:-3] + (A_HEADS, h, h))


def _ret_kernel(*refs, valid_rows, zero_init, n_alias):
    s0_ref = None
    if zero_init:
        zb_ref, cos_ref, sin_ref = refs[:3]
    else:
        zb_ref, cos_ref, sin_ref, s0_ref = refs[:4]
    yb_ref, s_ref = refs[-2:]

    @pl.when(pl.program_id(1) == 0)
    def _():
        _ret_init(s0_ref, s_ref)

    _ret_main(zb_ref, cos_ref, sin_ref, yb_ref, s_ref, valid_rows, 1)


def _ret_init(s0_ref, s_ref):
    if s0_ref is None:
        s_ref[...] = jnp.zeros(s_ref.shape, F32)
    else:
        s_ref[...] = s0_ref[0]


def _ret_main(zb_ref, cos_ref, sin_ref, yb_ref, s_ref, valid_rows, n_chunks):
    C = B_CHUNK
    half = B_KEY_DIM // 2
    ri = lax.broadcasted_iota(jnp.int32, (C, C), 0)
    rj = lax.broadcasted_iota(jnp.int32, (C, C), 1)
    diff = (ri - rj).astype(F32)
    idx = lax.broadcasted_iota(jnp.int32, (C, 1), 0).astype(F32)

    for i in range(n_chunks):
        rs = slice(i * C, (i + 1) * C)
        cos = cos_ref[rs, :]
        sin = sin_ref[rs, :]

        def rope(x):
            x1, x2 = x[:, :half], x[:, half:]
            return jnp.concatenate([x1 * cos - x2 * sin, x1 * sin + x2 * cos], axis=1)

        for h in range(B_HEADS):
            log_g = math.log(1.0 - 2.0 ** (-5.0 - h))
            q = rope(zb_ref[rs, h * B_KEY_DIM:(h + 1) * B_KEY_DIM].astype(F32))
            k = rope(zb_ref[rs, B_QK_WIDTH + h * B_KEY_DIM:B_QK_WIDTH + (h + 1) * B_KEY_DIM].astype(F32))
            k = k * (B_KEY_DIM ** -0.5)
            v = zb_ref[rs, 2 * B_QK_WIDTH + h * B_VAL_DIM:2 * B_QK_WIDTH + (h + 1) * B_VAL_DIM]
            g = zb_ref[rs, 2 * B_QK_WIDTH + B_V_WIDTH + h * B_VAL_DIM:
                       2 * B_QK_WIDTH + B_V_WIDTH + (h + 1) * B_VAL_DIM].astype(F32)
            mask = jnp.where(diff >= 0.0, jnp.exp(jnp.maximum(diff, 0.0) * log_g), 0.0)
            scores = _dot_nt(q, k) * mask
            s = s_ref[h]
            y = _dot(scores, v) + _dot(q * jnp.exp((idx + 1.0) * log_g), s)
            k_dec = jnp.where(idx < valid_rows, k * jnp.exp((valid_rows - 1.0 - idx) * log_g), 0.0)
            s_ref[h] = s * math.exp(valid_rows * log_g) + _dot(k_dec.T, v)
            yn = y * lax.rsqrt(jnp.mean(y * y, axis=-1, keepdims=True) + NORM_EPS)
            yb_ref[rs, h * B_VAL_DIM:(h + 1) * B_VAL_DIM] = (g * jax.nn.sigmoid(g) * yn).astype(yb_ref.dtype)


def _ret(zb, cos, sin, s0, bsz, n_chunks, valid_rows, layer, depth, prev):
    C = B_CHUNK
    half = B_KEY_DIM // 2
    zero_init = s0 is None
    in_specs = [pl.BlockSpec((C, B_COLS), lambda b, c: (b * n_chunks + c, 0)),
                pl.BlockSpec((C, half), lambda b, c: (c, 0)),
                pl.BlockSpec((C, half), lambda b, c: (c, 0))]
    args = [zb, cos, sin]
    if not zero_init:
        in_specs.append(pl.BlockSpec((None, 1, B_HEADS, B_KEY_DIM, B_VAL_DIM),
                                     lambda b, c: (layer, b, 0, 0, 0)))
        args.append(s0)
    aliases = {}
    if prev is not None:
        aliases = {len(args): 1}
        in_specs.append(pl.BlockSpec(memory_space=pl.ANY))
        args.append(prev)
    s_spec, s_shape = _stacked_out(layer, depth, bsz, (B_HEADS, B_KEY_DIM, B_VAL_DIM))
    return pl.pallas_call(
        functools.partial(_ret_kernel, valid_rows=valid_rows, zero_init=zero_init, n_alias=len(aliases)),
        grid=(bsz, n_chunks),
        in_specs=in_specs,
        out_specs=[pl.BlockSpec((C, B_V_WIDTH), lambda b, c: (b * n_chunks + c, 0)), s_spec],
        out_shape=[jax.ShapeDtypeStruct((bsz * n_chunks * C, B_V_WIDTH), BF16), s_shape],
        input_output_aliases=aliases,
        compiler_params=_params("arbitrary", "arbitrary"),
        name="ret",
    )(*args)


def _mixers_kernel(*refs, n_sub, n_ret, n_alias):
    za_ref, zb_ref, cos_ref, sin_ref = refs[:4]
    prm = refs[4:15]
    ya_ref, sp_ref, shift_out_ref, yb_ref, s_ref, carry_ref = refs[15 + n_alias:]

    @pl.when(pl.program_id(1) == 0)
    def _():
        _rwkv_init(None, None, sp_ref, carry_ref)
        _ret_init(None, s_ref)

    _rwkv_main(za_ref, prm, ya_ref, sp_ref, shift_out_ref, carry_ref, A_CHUNK, n_sub)
    _ret_main(zb_ref, cos_ref, sin_ref, yb_ref, s_ref, B_CHUNK, n_ret)


def _mixers(za, zb, cos, sin, pr, bsz, n_steps, layer, depth, prev):
    R = A_SUBCHUNKS * A_CHUNK
    half = B_KEY_DIM // 2
    rows = lambda w: pl.BlockSpec((R, w), lambda b, c: (b * n_steps + c, 0))
    vec = lambda n: pl.BlockSpec((1, n), lambda b, c: (0, 0))
    mat = lambda r, n: pl.BlockSpec((r, n), lambda b, c: (0, 0))
    in_specs = [rows(A_COLS), rows(B_COLS),
                pl.BlockSpec((R, half), lambda b, c: (c, 0)), pl.BlockSpec((R, half), lambda b, c: (c, 0)),
                vec(A_COLS), vec(A_WIDTH), mat(LANES, A_WIDTH), vec(A_WIDTH), mat(LANES, A_WIDTH),
                mat(A_GATE_RANK, A_WIDTH), vec(A_WIDTH), vec(A_WIDTH), vec(A_WIDTH),
                vec(A_WIDTH), vec(A_WIDTH)]
    args = [za, zb, cos, sin, pr["a_mu"], pr["a_w0"], pr["a_w2p"], pr["a_a0"], pr["a_a2p"], pr["a_g2"],
            pr["a_kk"], pr["a_ka"], pr["a_rk"], pr["a_ln_w"], pr["a_ln_b"]]
    aliases = {}
    if prev is not None:
        (sp_prev, shift_prev), ret_prev = prev
        aliases = {len(args): 1, len(args) + 1: 2, len(args) + 2: 4}
        in_specs += [pl.BlockSpec(memory_space=pl.ANY)] * 3
        args += [sp_prev, shift_prev, ret_prev]
    sp_spec, sp_shape = _stacked_out(layer, depth, bsz, (A_PAIRS, LANES, LANES))
    sh_spec, sh_shape = _stacked_out(layer, depth, bsz, (1, A_COLS))
    s_spec, s_shape = _stacked_out(layer, depth, bsz, (B_HEADS, B_KEY_DIM, B_VAL_DIM))
    ya, sp_all, shift_all, yb, ret_all = pl.pallas_call(
        functools.partial(_mixers_kernel, n_sub=A_SUBCHUNKS, n_ret=R // B_CHUNK, n_alias=len(aliases)),
        grid=(bsz, n_steps),
        in_specs=in_specs,
        out_specs=[rows(A_WIDTH), sp_spec, sh_spec, rows(B_V_WIDTH), s_spec],
        out_shape=[jax.ShapeDtypeStruct((bsz * n_steps * R, A_WIDTH), BF16), sp_shape, sh_shape,
                   jax.ShapeDtypeStruct((bsz * n_steps * R, B_V_WIDTH), BF16), s_shape],
        scratch_shapes=[pltpu.VMEM((1, A_COLS), F32)],
        input_output_aliases=aliases,
        compiler_params=_params("arbitrary", "arbitrary"),
        name="mixers",
    )(*args)
    return ya, yb, ((sp_all, shift_all), ret_all)


def _merge_kernel(ya_ref, yb_ref, zg_ref, x_ref, gt_ref, g_ref, wa_ref, wb_ref, wo_ref, o_ref):
    pa = jnp.dot(ya_ref[...], wa_ref[...], preferred_element_type=F32)
    pb = jnp.dot(yb_ref[...], wb_ref[...], preferred_element_type=F32)
    ga = jax.nn.sigmoid(zg_ref[:, :D_MODEL].astype(F32))
    gb = jax.nn.sigmoid(zg_ref[:, D_MODEL:].astype(F32))
    out = _dot(ga * pa + gb * pb, wo_ref[...])
    o_ref[...] = x_ref[...] + gt_ref[0] * _rms(out, g_ref[...])


def _resident(shape):
    return pl.BlockSpec(shape, lambda i: (0,) * len(shape), pipeline_mode=pl.Buffered(1))


def _merge(ya, yb, zg, x, gt, g, wa, wb, wo, tiles_per_group):
    n, d = x.shape
    rows = lambda w: pl.BlockSpec((ROW_TILE, w), lambda i: (i, 0))
    return pl.pallas_call(
        _merge_kernel,
        grid=(n // ROW_TILE,),
        in_specs=[rows(A_WIDTH), rows(B_V_WIDTH), rows(GATE_COLS), rows(d),
                  _mod_spec(gt, tiles_per_group), _resident((1, d)),
                  _resident(wa.shape), _resident(wb.shape), _resident(wo.shape)],
        out_specs=rows(d),
        out_shape=jax.ShapeDtypeStruct((n, d), F32),
        compiler_params=_params("arbitrary"),
        name="merge",
    )(ya, yb, zg, x, gt, g, wa, wb, wo)


def _ffn_kernel(x_ref, sc_ref, sh_ref, gt_ref, gpre_ref, gpost_ref, up_ref, down_ref, *rest):
    x = x_ref[...]
    h = (_rms(x, gpre_ref[...]) * (1.0 + sc_ref[0]) + sh_ref[0]).astype(BF16)
    f = jnp.zeros(x.shape, F32)
    for j in range(FFN_HIDDEN // FFN_SLAB):
        sl = slice(j * FFN_SLAB, (j + 1) * FFN_SLAB)
        a = jnp.maximum(jnp.dot(h, up_ref[:, sl], preferred_element_type=F32), 0.0)
        f = f + _dot(a * a, down_ref[sl, :])
    out = x + gt_ref[0] * _rms(f, gpost_ref[...])
    if len(rest) == 1:
        (o_ref,) = rest
    else:
        gn_ref, scn_ref, shn_ref, o_ref, hn_ref = rest
        hn_ref[...] = (_rms(out, gn_ref[...]) * (1.0 + scn_ref[0]) + shn_ref[0]).astype(hn_ref.dtype)
    o_ref[...] = out


def _ffn(x, sc, sh, gt, gpre, gpost, up, down, tiles_per_group, nxt):
    n, d = x.shape
    rows = pl.BlockSpec((ROW_TILE, d), lambda i: (i, 0))
    mod = _mod_spec(sc, tiles_per_group)
    in_specs = [rows, mod, mod, mod, _resident((1, d)), _resident((1, d)),
                _resident(up.shape), _resident(down.shape)]
    args = [x, sc, sh, gt, gpre, gpost, up, down]
    out_specs, out_shape = [rows], [jax.ShapeDtypeStruct((n, d), F32)]
    if nxt is not None:
        in_specs += [_resident((1, d)), mod, mod]
        args += list(nxt)
        out_specs.append(rows)
        out_shape.append(jax.ShapeDtypeStruct((n, d), BF16))
    outs = pl.pallas_call(
        _ffn_kernel,
        grid=(n // ROW_TILE,),
        in_specs=in_specs,
        out_specs=out_specs,
        out_shape=out_shape,
        compiler_params=_params("arbitrary"),
        name="ffn",
    )(*args)
    return outs if nxt is not None else (outs[0], None)


def _pad_frames(z, bsz, t, t_pad):
    if t_pad == t:
        return z
    z = z.reshape(bsz, t, -1)
    return jnp.pad(z, ((0, 0), (0, t_pad - t), (0, 0))).reshape(bsz * t_pad, -1)


def _drop_frames(y, bsz, t, t_pad):
    if t_pad == t:
        return y
    return y.reshape(bsz, t_pad, -1)[:, :t].reshape(bsz * t, -1)


def _mods(mod, bsz, t):
    if t % ROW_TILE == 0:
        return [m.reshape(bsz, 1, D_MODEL) for m in jnp.split(mod, 6, axis=-1)], t // ROW_TILE
    n = bsz * t
    return [jnp.repeat(m, t, axis=0).reshape(n // ROW_TILE, ROW_TILE, D_MODEL)
            for m in jnp.split(mod, 6, axis=-1)], 1


def _layer(x, h, mods, nxt, shift0, sp0, ret0, pos0, pr, bsz, t, layer, depth, prev):
    (sh1, sc1, gt1, sh2, sc2, gt2), tiles_per_group = mods
    if h is None:
        h = _prenorm(x, pr["norm_pre0"], sc1, sh1, tiles_per_group)
    za = _matmul(h, pr["w_in_a"], A_COLS, "in_proj_a")
    zb = _matmul(h, pr["w_in_b"], B_COLS // 4, "in_proj_b")
    zg = _matmul(h, pr["w_in_g"], GATE_COLS, "in_proj_g")

    half = B_KEY_DIM // 2
    inv_freq = ROPE_BASE ** (-jnp.linspace(0.0, 1.0, half, dtype=F32))
    ra = A_SUBCHUNKS * A_CHUNK
    if shift0 is None and t % ra == 0:
        ang = (pos0 + jnp.arange(t, dtype=jnp.int32)).astype(F32)[:, None] * inv_freq[None, :]
        ya, yb, states = _mixers(za, zb, jnp.cos(ang), jnp.sin(ang), pr, bsz, t // ra, layer, depth, prev)
    else:
        ta = -(-t // A_CHUNK) * A_CHUNK
        ya, rwkv_states = _rwkv(_pad_frames(za, bsz, t, ta), shift0, sp0, pr, bsz, ta // A_CHUNK, 1,
                                min(t, A_CHUNK), layer, depth, None if prev is None else prev[0])
        ya = _drop_frames(ya, bsz, t, ta)
        tb = -(-t // B_CHUNK) * B_CHUNK
        ang = (pos0 + jnp.arange(tb, dtype=jnp.int32)).astype(F32)[:, None] * inv_freq[None, :]
        yb, ret_states = _ret(_pad_frames(zb, bsz, t, tb), jnp.cos(ang), jnp.sin(ang), ret0, bsz,
                              tb // B_CHUNK, min(t, B_CHUNK), layer, depth, None if prev is None else prev[1])
        yb = _drop_frames(yb, bsz, t, tb)
        states = (rwkv_states, ret_states)

    x = _merge(ya, yb, zg, x, gt1, pr["norm_post0"], pr["w_branch_a"], pr["w_branch_b"], pr["w_out"],
               tiles_per_group)
    x, h_next = _ffn(x, sc2, sh2, gt2, pr["norm_pre1"], pr["norm_post1"], pr["w_ffn_up"], pr["w_ffn_down"],
                     tiles_per_group, nxt)
    return x, h_next, states


def kernel(x_prompt, x_sample, state_shift, state_rwkv, state_ret, c_prompt, c_sample, ada_w, ada_b, norm_pre, norm_post, w_in, a_mu, a_w0, a_w2, a_a0, a_a2, a_g2, a_kk, a_ka, a_rk, a_ln_w, a_ln_b, w_branch_a, w_branch_b, w_out, w_ffn_up, w_ffn_down):
    depth = w_in.shape[0]
    bp, tp, _ = x_prompt.shape
    bs, ts, _ = x_sample.shape
    xp = x_prompt.reshape(bp * tp, D_MODEL)
    xs = x_sample.reshape(bs * ts, D_MODEL)
    c_all = jnp.concatenate([c_prompt, c_sample], axis=0)
    row = lambda a: a.reshape(1, -1)
    lora_pad = jnp.zeros((A_DECAY_RANK, A_WIDTH), BF16)

    mods_p, mods_s = [], []
    for l in range(depth):
        mod = _ada(c_all, ada_w[l].astype(BF16), ada_b[l])
        mods_p.append(_mods(mod[:bp], bp, tp))
        mods_s.append(_mods(mod[bp:], bs, ts))

    def next_inputs(mods, l):
        if l + 1 == depth:
            return None
        (sh1, sc1, _, _, _, _), _ = mods[l + 1]
        return row(norm_pre[l + 1, 0]), sc1, sh1

    prev_p = prev_s = hp = hs = None
    for l in range(depth):
        wl = w_in[l].astype(BF16)
        pr = {
            "norm_pre0": row(norm_pre[l, 0]), "norm_pre1": row(norm_pre[l, 1]),
            "norm_post0": row(norm_post[l, 0]), "norm_post1": row(norm_post[l, 1]),
            "w_in_a": wl[:, :A_COLS], "w_in_b": wl[:, A_COLS:A_COLS + B_COLS],
            "w_in_g": wl[:, A_COLS + B_COLS:],
            "a_mu": row(a_mu[l]), "a_w0": row(a_w0[l]), "a_a0": row(a_a0[l]),
            "a_w2p": jnp.concatenate([a_w2[l].astype(BF16), lora_pad], axis=0),
            "a_a2p": jnp.concatenate([lora_pad, a_a2[l].astype(BF16)], axis=0),
            "a_g2": a_g2[l].astype(BF16),
            "a_kk": row(a_kk[l]), "a_ka": row(a_ka[l]), "a_rk": row(a_rk[l]),
            "a_ln_w": row(a_ln_w[l]), "a_ln_b": row(a_ln_b[l]),
            "w_branch_a": w_branch_a[l].astype(BF16), "w_branch_b": w_branch_b[l].astype(BF16),
            "w_out": w_out[l].astype(BF16),
            "w_ffn_up": w_ffn_up[l].astype(BF16), "w_ffn_down": w_ffn_down[l].astype(BF16),
        }
        xp, hp, prev_p = _layer(xp, hp, mods_p[l], next_inputs(mods_p, l), None, None, None, 0, pr,
                                bp, tp, l, depth, prev_p)
        xs, hs, prev_s = _layer(xs, hs, mods_s[l], next_inputs(mods_s, l),
                                state_shift[l].reshape(bs, 1, A_COLS), _pack_pairs(state_rwkv[l]),
                                state_ret, PAST_LEN, pr, bs, ts, l, depth, prev_s)

    def states(prev, bsz):
        (sp_all, shift_all), ret_all = prev
        return shift_all.reshape(depth, bsz, A_COLS), _unpack_pairs(sp_all), ret_all

    return (xp.reshape(bp, tp, D_MODEL), xs.reshape(bs, ts, D_MODEL), *states(prev_p, bp), *states(prev_s, bs))
```

```python
import functools
import math

import jax
import jax.numpy as jnp
from jax import lax
from jax.experimental import pallas as pl
from jax.experimental.pallas import tpu as pltpu

F32 = jnp.float32
BF16 = jnp.bfloat16

D_MODEL = 1024
A_HEADS = 16
A_HEAD_DIM = 64
A_WIDTH = A_HEADS * A_HEAD_DIM
A_DECAY_RANK = 64
A_ICLR_RANK = 64
A_GATE_RANK = 128
A_COLS = 3 * A_WIDTH + A_DECAY_RANK + A_ICLR_RANK + A_GATE_RANK
A_GN_EPS = 64e-5
B_HEADS = 4
B_KEY_DIM = 256
B_VAL_DIM = 512
B_QK_WIDTH = B_HEADS * B_KEY_DIM
B_V_WIDTH = B_HEADS * B_VAL_DIM
B_COLS = 2 * B_QK_WIDTH + 2 * B_V_WIDTH
GATE_COLS = 2 * D_MODEL
FFN_HIDDEN = 4 * D_MODEL
ROPE_BASE = 10000.0
NORM_EPS = 1e-6
PAST_LEN = 4096

LANES = 128
MXU_DIM = 256
A_PAIRS = A_WIDTH // LANES
A_CHUNK = 64
A_SUBCHUNKS = 4
B_CHUNK = 128
ROW_TILE = 512
FFN_SLAB = 1024
VMEM_LIMIT = 56 * 1024 * 1024


def _dot(a, b):
    return jnp.dot(a.astype(BF16), b.astype(BF16), preferred_element_type=F32)


def _dot_nt(a, b):
    return lax.dot_general(a.astype(BF16), b.astype(BF16), (((1,), (1,)), ((), ())),
                           preferred_element_type=F32)


def _rms(x, g):
    return x * lax.rsqrt(jnp.mean(x * x, axis=-1, keepdims=True) + NORM_EPS) * g


def _params(*sem):
    return pltpu.CompilerParams(dimension_semantics=sem, vmem_limit_bytes=VMEM_LIMIT)


def _ada_kernel(c_ref, w_ref, b_ref, o_ref):
    c = c_ref[...]
    o_ref[...] = _dot(c * jax.nn.sigmoid(c), w_ref[...]) + b_ref[...]


def _ada(c, w, b):
    n, d = c.shape
    cols = w.shape[1]
    tn = cols // 4
    return pl.pallas_call(
        _ada_kernel,
        grid=(cols // tn,),
        in_specs=[pl.BlockSpec((n, d), lambda j: (0, 0)),
                  pl.BlockSpec((d, tn), lambda j: (0, j)),
                  pl.BlockSpec((1, tn), lambda j: (0, j))],
        out_specs=pl.BlockSpec((n, tn), lambda j: (0, j)),
        out_shape=jax.ShapeDtypeStruct((n, cols), F32),
        compiler_params=_params("arbitrary"),
        name="ada",
    )(c, w, b.reshape(1, cols))


def _prenorm_kernel(x_ref, g_ref, sc_ref, sh_ref, o_ref):
    h = _rms(x_ref[...], g_ref[...]) * (1.0 + sc_ref[0]) + sh_ref[0]
    o_ref[...] = h.astype(o_ref.dtype)


def _mod_spec(mod, tiles_per_group):
    _, r, d = mod.shape
    return pl.BlockSpec((1, r, d), lambda i: (i // tiles_per_group, 0, 0))


def _prenorm(x, g, sc, sh, tiles_per_group):
    n, d = x.shape
    return pl.pallas_call(
        _prenorm_kernel,
        grid=(n // ROW_TILE,),
        in_specs=[pl.BlockSpec((ROW_TILE, d), lambda i: (i, 0)),
                  pl.BlockSpec((1, d), lambda i: (0, 0)),
                  _mod_spec(sc, tiles_per_group), _mod_spec(sh, tiles_per_group)],
        out_specs=pl.BlockSpec((ROW_TILE, d), lambda i: (i, 0)),
        out_shape=jax.ShapeDtypeStruct((n, d), BF16),
        compiler_params=_params("arbitrary"),
        name="prenorm",
    )(x, g, sc, sh)


def _matmul_kernel(a_ref, w_ref, o_ref):
    o_ref[...] = jnp.dot(a_ref[...], w_ref[...], preferred_element_type=F32).astype(o_ref.dtype)


def _matmul(a, w, tn, name):
    n, k = a.shape
    cols = w.shape[1]
    tm = min(n, 1024)
    return pl.pallas_call(
        _matmul_kernel,
        grid=(cols // tn, n // tm),
        in_specs=[pl.BlockSpec((tm, k), lambda j, i: (i, 0)),
                  pl.BlockSpec((k, tn), lambda j, i: (0, j))],
        out_specs=pl.BlockSpec((tm, tn), lambda j, i: (i, j)),
        out_shape=jax.ShapeDtypeStruct((n, cols), BF16),
        compiler_params=_params("arbitrary", "arbitrary"),
        name=name,
    )(a, w)


def _split3(x):
    h1 = x.astype(BF16)
    r1 = x - h1.astype(F32)
    h2 = r1.astype(BF16)
    h3 = (r1 - h2.astype(F32)).astype(BF16)
    return h1, h2, h3


def _head_sum(x, ones_bd):
    c = x.shape[0]
    groups = A_WIDTH // MXU_DIM
    xs = jnp.concatenate([x[:, q * MXU_DIM:(q + 1) * MXU_DIM] for q in range(groups)], axis=0)
    s = jnp.dot(xs.astype(BF16), ones_bd, preferred_element_type=F32)
    return jnp.concatenate([s[q * c:(q + 1) * c] for q in range(groups)], axis=1)


def _rwkv_init(shift0_ref, sp0_ref, sp_ref, carry_ref):
    if shift0_ref is None:
        carry_ref[...] = jnp.zeros(carry_ref.shape, F32)
        sp_ref[...] = jnp.zeros(sp_ref.shape, F32)
    else:
        carry_ref[...] = shift0_ref[0]
        sp_ref[...] = sp0_ref[0]


def _rwkv_kernel(*refs, valid_rows, n_sub, zero_init, n_alias):
    shift0_ref = sp0_ref = None
    if zero_init:
        za_ref = refs[0]
        refs = refs[1:]
    else:
        za_ref, shift0_ref, sp0_ref = refs[:3]
        refs = refs[3:]
    ya_ref, sp_ref, shift_out_ref, carry_ref = refs[11 + n_alias:]

    @pl.when(pl.program_id(1) == 0)
    def _():
        _rwkv_init(shift0_ref, sp0_ref, sp_ref, carry_ref)

    _rwkv_main(za_ref, refs[:11], ya_ref, sp_ref, shift_out_ref, carry_ref, valid_rows, n_sub)


def _rwkv_main(za_ref, prm, ya_ref, sp_ref, shift_out_ref, carry_ref, valid_rows, n_sub):
    mu_ref, w0_ref, w2_ref, a0_ref, a2_ref, g2_ref, kkp_ref, kap_ref, rk_ref, lnw_ref, lnb_ref = prm
    C = A_CHUNK
    C2 = 2 * C
    R = n_sub * C

    za = za_ref[...].astype(F32)
    row = lax.broadcasted_iota(jnp.int32, (R, 1), 0)
    za_prev = jnp.where(row == 0, carry_ref[...], pltpu.roll(za, 1, axis=0))
    carry_ref[...] = za[R - 1:R, :]
    last = R - C + valid_rows - 1
    shift_out_ref[...] = za[last:last + 1, :]
    mix = za + (za_prev - za) * mu_ref[...]

    r = mix[:, 0:A_WIDTH]
    k = mix[:, A_WIDTH:2 * A_WIDTH]
    v = mix[:, 2 * A_WIDTH:3 * A_WIDTH]
    zwi = mix[:, 3 * A_WIDTH:3 * A_WIDTH + LANES]
    zg = mix[:, 3 * A_WIDTH + LANES:A_COLS]

    lw = -math.exp(-0.5) * jax.nn.sigmoid(w0_ref[...] + _dot(jnp.tanh(zwi), w2_ref[...]))
    iclr = jax.nn.sigmoid(a0_ref[...] + _dot(zwi, a2_ref[...]))
    gate = _dot(jax.nn.sigmoid(zg), g2_ref[...])

    li = lax.broadcasted_iota(jnp.int32, (LANES, LANES), 0)
    lj = lax.broadcasted_iota(jnp.int32, (LANES, LANES), 1)
    blk_diag = (li // A_HEAD_DIM) == (lj // A_HEAD_DIM)
    mi = lax.broadcasted_iota(jnp.int32, (MXU_DIM, MXU_DIM), 0)
    mj = lax.broadcasted_iota(jnp.int32, (MXU_DIM, MXU_DIM), 1)
    ones_bd = jnp.where((mi // A_HEAD_DIM) == (mj // A_HEAD_DIM), 1.0, 0.0).astype(BF16)

    kk = k * kkp_ref[...]
    kk = kk * lax.rsqrt(jnp.maximum(_head_sum(kk * kk, ones_bd), 1e-24))
    k2 = k * (1.0 + (iclr - 1.0) * kap_ref[...])
    if valid_rows < C:
        valid = row < valid_rows
        lw = jnp.where(valid, lw, 0.0)
        kk = jnp.where(valid, kk, 0.0)
        k2 = jnp.where(valid, k2, 0.0)
    b = kk * iclr

    ti = lax.broadcasted_iota(jnp.int32, (R, R), 0)
    tj = lax.broadcasted_iota(jnp.int32, (R, R), 1)
    tril_incl = jnp.where((ti >= tj) & ((ti // C) == (tj // C)), 1.0, 0.0).astype(BF16)
    cum = sum(jnp.dot(tril_incl, part, preferred_element_type=F32) for part in _split3(lw))
    e_p = jnp.exp(cum)
    e_m = jnp.exp(-cum)
    e_tots = [e_p[j * C + C - 1:j * C + C, :] for j in range(n_sub)]
    e_tot_rows = jnp.concatenate([jnp.broadcast_to(e, (C, A_WIDTH)) for e in e_tots], axis=0)
    rt = r * e_p
    at = -kk * jnp.exp(cum - lw)
    bt = b * e_m
    kt = k2 * e_m
    kh = kt * e_tot_rows
    bh = bt * e_tot_rows

    gi = lax.broadcasted_iota(jnp.int32, (2 * C2, C2), 0)
    gj = lax.broadcasted_iota(jnp.int32, (2 * C2, C2), 1)
    blk = gi // C
    lower = (gi % C) >= (gj % C)
    mask_g = lower & (((blk == 1) | (blk == 2)) | ((gi % C) != (gj % C)))
    qi = lax.broadcasted_iota(jnp.int32, (C2, C2), 0)
    qj = lax.broadcasted_iota(jnp.int32, (C2, C2), 1)
    blk_tl = (qi < C) & (qj < C)
    blk_br = (qi >= C) & (qj >= C)
    eye2 = jnp.where(qi == qj, 1.0, 0.0).astype(F32)
    lane = lax.broadcasted_iota(jnp.int32, (1, LANES), 1)
    m0 = lane < A_HEAD_DIM
    zeros = jnp.zeros((C, LANES), F32)
    n_double = int(math.log2(C)) - 1

    pairs = range(A_PAIRS)
    lanes = [slice(p * LANES, (p + 1) * LANES) for p in pairs]
    rows = [slice(j * C, (j + 1) * C) for j in range(n_sub)]
    streams = [(rw, ln) for rw in rows for ln in lanes]
    cat = lambda *xs: jnp.concatenate(xs, axis=0)
    only0 = lambda x: jnp.where(m0, x, 0.0)
    only1 = lambda x: jnp.where(m0, 0.0, x)
    split01 = lambda x: jnp.where(m0, x[0:C], x[C:C2])
    H = C // 2
    zeros_h = jnp.zeros((H, C2), F32)
    later = lambda x: cat(x[H:C], x[C + H:C2])
    spread = lambda x: cat(zeros_h, x[0:H], zeros_h, x[H:C])
    def state_free(st):
        gs = [jnp.where(mask_g, _dot(cat(only0(at[s]), only0(rt[s]), only1(rt[s]), only1(at[s])),
                                     cat(bt[s], kt[s]).T), 0.0) for s in st]
        pws = [jnp.where(blk_tl, g[0:C2], jnp.where(blk_br, pltpu.roll(g[C2:2 * C2], C, axis=1), 0.0))
               for g in gs]
        tts = [eye2 + nn for nn in pws]
        pws = [_dot(pw, pw) for pw in pws]
        for _ in range(n_double - 2):
            both = [_dot(cat(pw, tt), pw) for pw, tt in zip(pws, tts)]
            pws = [bth[0:C2] for bth in both]
            tts = [tt + bth[C2:2 * C2] for tt, bth in zip(tts, both)]
        both = [_dot(cat(later(pw), tt), pw) for pw, tt in zip(pws, tts)]
        pws = [bth[0:C] for bth in both]
        tts = [tt + bth[C:C + C2] for tt, bth in zip(tts, both)]
        tts = [tt + spread(_dot(later(tt), spread(pw))) for tt, pw in zip(tts, pws)]
        t01s = [tt[0:C] + tt[C:C2] for tt in tts]
        akv = [split01(_dot(cat(g[0:C], g[3 * C:4 * C]), cat(zeros, v[s]))) for g, s in zip(gs, st)]
        return gs, t01s, akv

    def with_state(j, st, free, sps):
        gs, t01s, akv = free
        from_state = [_dot(cat(at[s], rt[s]), sp.T) for s, sp in zip(st, sps)]
        ws = [fs[0:C] + x for fs, x in zip(from_state, akv)]
        us = [_dot(t01, cat(only0(w), only1(w))) for t01, w in zip(t01s, ws)]
        ys = [fs[C:C2] + split01(_dot(g[C:3 * C], cat(u, v[s])))
              for fs, g, s, u in zip(from_state, gs, st, us)]
        sps = [jnp.where(blk_diag, sp * e_tots[j][:, s[1]] + _dot(cat(v[s], u).T, cat(kh[s], bh[s])), 0.0)
               for s, sp, u in zip(st, sps, us)]
        return ys, sps

    chunk_streams = [streams[j * A_PAIRS:(j + 1) * A_PAIRS] for j in range(n_sub)]
    sps = [sp_ref[p] for p in pairs]
    ys = []
    free = state_free(chunk_streams[0])
    for j in range(n_sub):
        nxt = state_free(chunk_streams[j + 1]) if j + 1 < n_sub else None
        yj, sps = with_state(j, chunk_streams[j], free, sps)
        ys.append(yj)
        free = nxt
    for p in pairs:
        sp_ref[p] = sps[p]

    y = jnp.concatenate([jnp.concatenate(yj, axis=1) for yj in ys], axis=0)
    mu = _head_sum(y, ones_bd) * (1.0 / A_HEAD_DIM)
    d = y - mu
    var = _head_sum(d * d, ones_bd) * (1.0 / A_HEAD_DIM)
    yn = d * lax.rsqrt(var + A_GN_EPS) * lnw_ref[...] + lnb_ref[...]
    bonus = _head_sum(r * k2 * rk_ref[...], ones_bd) * v
    ya_ref[...] = ((yn + bonus) * gate).astype(ya_ref.dtype)


def _stacked_out(layer, depth, bsz, tail):
    spec = pl.BlockSpec((None, None) + tail, lambda b, c: (layer, b) + (0,) * len(tail))
    return spec, jax.ShapeDtypeStruct((depth, bsz) + tail, F32)


def _rwkv(za, shift0, sp0, pr, bsz, n_steps, n_sub, valid_rows, layer, depth, prev):
    R = n_sub * A_CHUNK
    zero_init = shift0 is None
    vec = lambda n: pl.BlockSpec((1, n), lambda b, c: (0, 0))
    mat = lambda r, n: pl.BlockSpec((r, n), lambda b, c: (0, 0))
    in_specs = [pl.BlockSpec((R, A_COLS), lambda b, c: (b * n_steps + c, 0))]
    args = [za]
    if not zero_init:
        in_specs += [pl.BlockSpec((1, 1, A_COLS), lambda b, c: (b, 0, 0)),
                     pl.BlockSpec((1, A_PAIRS, LANES, LANES), lambda b, c: (b, 0, 0, 0))]
        args += [shift0, sp0]
    in_specs += [vec(A_COLS), vec(A_WIDTH), mat(LANES, A_WIDTH), vec(A_WIDTH), mat(LANES, A_WIDTH),
                 mat(A_GATE_RANK, A_WIDTH), vec(A_WIDTH), vec(A_WIDTH), vec(A_WIDTH),
                 vec(A_WIDTH), vec(A_WIDTH)]
    args += [pr["a_mu"], pr["a_w0"], pr["a_w2p"], pr["a_a0"], pr["a_a2p"], pr["a_g2"],
             pr["a_kk"], pr["a_ka"], pr["a_rk"], pr["a_ln_w"], pr["a_ln_b"]]
    aliases = {}
    if prev is not None:
        aliases = {len(args): 1, len(args) + 1: 2}
        in_specs += [pl.BlockSpec(memory_space=pl.ANY)] * 2
        args += list(prev)
    sp_spec, sp_shape = _stacked_out(layer, depth, bsz, (A_PAIRS, LANES, LANES))
    sh_spec, sh_shape = _stacked_out(layer, depth, bsz, (1, A_COLS))
    ya, sp_all, shift_all = pl.pallas_call(
        functools.partial(_rwkv_kernel, valid_rows=valid_rows, n_sub=n_sub, zero_init=zero_init,
                          n_alias=len(aliases)),
        grid=(bsz, n_steps),
        in_specs=in_specs,
        out_specs=[pl.BlockSpec((R, A_WIDTH), lambda b, c: (b * n_steps + c, 0)), sp_spec, sh_spec],
        out_shape=[jax.ShapeDtypeStruct((bsz * n_steps * R, A_WIDTH), BF16), sp_shape, sh_shape],
        scratch_shapes=[pltpu.VMEM((1, A_COLS), F32)],
        input_output_aliases=aliases,
        compiler_params=_params("arbitrary", "arbitrary"),
        name="rwkv",
    )(*args)
    return ya, (sp_all, shift_all)


def _pack_pairs(s):
    bsz = s.shape[0]
    s = s.reshape(bsz, A_PAIRS, 2, A_HEAD_DIM, A_HEAD_DIM)
    z = jnp.zeros_like(s[:, :, 0])
    top = jnp.concatenate([s[:, :, 0], z], axis=-1)
    bot = jnp.concatenate([z, s[:, :, 1]], axis=-1)
    return jnp.concatenate([top, bot], axis=-2)


def _unpack_pairs(sp):
    h = A_HEAD_DIM
    s = jnp.stack([sp[..., :h, :h], sp[..., h:, h:]], axis=-3)
    return s.reshape(sp.shape[:-3] + (A_HEADS, h, h))


def _ret_kernel(*refs, valid_rows, zero_init, n_alias):
    s0_ref = None
    if zero_init:
        zb_ref, cos_ref, sin_ref = refs[:3]
    else:
        zb_ref, cos_ref, sin_ref, s0_ref = refs[:4]
    yb_ref, s_ref = refs[-2:]

    @pl.when(pl.program_id(1) == 0)
    def _():
        _ret_init(s0_ref, s_ref)

    _ret_main(zb_ref, cos_ref, sin_ref, yb_ref, s_ref, valid_rows, 1)


def _ret_init(s0_ref, s_ref):
    if s0_ref is None:
        s_ref[...] = jnp.zeros(s_ref.shape, F32)
    else:
        s_ref[...] = s0_ref[0]


def _ret_main(zb_ref, cos_ref, sin_ref, yb_ref, s_ref, valid_rows, n_chunks):
    C = B_CHUNK
    half = B_KEY_DIM // 2
    ri = lax.broadcasted_iota(jnp.int32, (C, C), 0)
    rj = lax.broadcasted_iota(jnp.int32, (C, C), 1)
    diff = (ri - rj).astype(F32)
    idx = lax.broadcasted_iota(jnp.int32, (C, 1), 0).astype(F32)

    for i in range(n_chunks):
        rs = slice(i * C, (i + 1) * C)
        cos = cos_ref[rs, :]
        sin = sin_ref[rs, :]

        def rope(x):
            x1, x2 = x[:, :half], x[:, half:]
            return jnp.concatenate([x1 * cos - x2 * sin, x1 * sin + x2 * cos], axis=1)

        for h in range(B_HEADS):
            log_g = math.log(1.0 - 2.0 ** (-5.0 - h))
            q = rope(zb_ref[rs, h * B_KEY_DIM:(h + 1) * B_KEY_DIM].astype(F32))
            k = rope(zb_ref[rs, B_QK_WIDTH + h * B_KEY_DIM:B_QK_WIDTH + (h + 1) * B_KEY_DIM].astype(F32))
            k = k * (B_KEY_DIM ** -0.5)
            v = zb_ref[rs, 2 * B_QK_WIDTH + h * B_VAL_DIM:2 * B_QK_WIDTH + (h + 1) * B_VAL_DIM]
            g = zb_ref[rs, 2 * B_QK_WIDTH + B_V_WIDTH + h * B_VAL_DIM:
                       2 * B_QK_WIDTH + B_V_WIDTH + (h + 1) * B_VAL_DIM].astype(F32)
            mask = jnp.where(diff >= 0.0, jnp.exp(jnp.maximum(diff, 0.0) * log_g), 0.0)
            scores = _dot_nt(q, k) * mask
            s = s_ref[h]
            y = _dot(scores, v) + _dot(q * jnp.exp((idx + 1.0) * log_g), s)
            k_dec = jnp.where(idx < valid_rows, k * jnp.exp((valid_rows - 1.0 - idx) * log_g), 0.0)
            s_ref[h] = s * math.exp(valid_rows * log_g) + _dot(k_dec.T, v)
            yn = y * lax.rsqrt(jnp.mean(y * y, axis=-1, keepdims=True) + NORM_EPS)
            yb_ref[rs, h * B_VAL_DIM:(h + 1) * B_VAL_DIM] = (g * jax.nn.sigmoid(g) * yn).astype(yb_ref.dtype)


def _ret(zb, cos, sin, s0, bsz, n_chunks, valid_rows, layer, depth, prev):
    C = B_CHUNK
    half = B_KEY_DIM // 2
    zero_init = s0 is None
    in_specs = [pl.BlockSpec((C, B_COLS), lambda b, c: (b * n_chunks + c, 0)),
                pl.BlockSpec((C, half), lambda b, c: (c, 0)),
                pl.BlockSpec((C, half), lambda b, c: (c, 0))]
    args = [zb, cos, sin]
    if not zero_init:
        in_specs.append(pl.BlockSpec((None, 1, B_HEADS, B_KEY_DIM, B_VAL_DIM),
                                     lambda b, c: (layer, b, 0, 0, 0)))
        args.append(s0)
    aliases = {}
    if prev is not None:
        aliases = {len(args): 1}
        in_specs.append(pl.BlockSpec(memory_space=pl.ANY))
        args.append(prev)
    s_spec, s_shape = _stacked_out(layer, depth, bsz, (B_HEADS, B_KEY_DIM, B_VAL_DIM))
    return pl.pallas_call(
        functools.partial(_ret_kernel, valid_rows=valid_rows, zero_init=zero_init, n_alias=len(aliases)),
        grid=(bsz, n_chunks),
        in_specs=in_specs,
        out_specs=[pl.BlockSpec((C, B_V_WIDTH), lambda b, c: (b * n_chunks + c, 0)), s_spec],
        out_shape=[jax.ShapeDtypeStruct((bsz * n_chunks * C, B_V_WIDTH), BF16), s_shape],
        input_output_aliases=aliases,
        compiler_params=_params("arbitrary", "arbitrary"),
        name="ret",
    )(*args)


def _mixers_kernel(*refs, n_sub, n_ret, n_alias):
    za_ref, zb_ref, cos_ref, sin_ref = refs[:4]
    prm = refs[4:15]
    ya_ref, sp_ref, shift_out_ref, yb_ref, s_ref, carry_ref = refs[15 + n_alias:]

    @pl.when(pl.program_id(1) == 0)
    def _():
        _rwkv_init(None, None, sp_ref, carry_ref)
        _ret_init(None, s_ref)

    _rwkv_main(za_ref, prm, ya_ref, sp_ref, shift_out_ref, carry_ref, A_CHUNK, n_sub)
    _ret_main(zb_ref, cos_ref, sin_ref, yb_ref, s_ref, B_CHUNK, n_ret)


def _mixers(za, zb, cos, sin, pr, bsz, n_steps, layer, depth, prev):
    R = A_SUBCHUNKS * A_CHUNK
    half = B_KEY_DIM // 2
    rows = lambda w: pl.BlockSpec((R, w), lambda b, c: (b * n_steps + c, 0))
    vec = lambda n: pl.BlockSpec((1, n), lambda b, c: (0, 0))
    mat = lambda r, n: pl.BlockSpec((r, n), lambda b, c: (0, 0))
    in_specs = [rows(A_COLS), rows(B_COLS),
                pl.BlockSpec((R, half), lambda b, c: (c, 0)), pl.BlockSpec((R, half), lambda b, c: (c, 0)),
                vec(A_COLS), vec(A_WIDTH), mat(LANES, A_WIDTH), vec(A_WIDTH), mat(LANES, A_WIDTH),
                mat(A_GATE_RANK, A_WIDTH), vec(A_WIDTH), vec(A_WIDTH), vec(A_WIDTH),
                vec(A_WIDTH), vec(A_WIDTH)]
    args = [za, zb, cos, sin, pr["a_mu"], pr["a_w0"], pr["a_w2p"], pr["a_a0"], pr["a_a2p"], pr["a_g2"],
            pr["a_kk"], pr["a_ka"], pr["a_rk"], pr["a_ln_w"], pr["a_ln_b"]]
    aliases = {}
    if prev is not None:
        (sp_prev, shift_prev), ret_prev = prev
        aliases = {len(args): 1, len(args) + 1: 2, len(args) + 2: 4}
        in_specs += [pl.BlockSpec(memory_space=pl.ANY)] * 3
        args += [sp_prev, shift_prev, ret_prev]
    sp_spec, sp_shape = _stacked_out(layer, depth, bsz, (A_PAIRS, LANES, LANES))
    sh_spec, sh_shape = _stacked_out(layer, depth, bsz, (1, A_COLS))
    s_spec, s_shape = _stacked_out(layer, depth, bsz, (B_HEADS, B_KEY_DIM, B_VAL_DIM))
    ya, sp_all, shift_all, yb, ret_all = pl.pallas_call(
        functools.partial(_mixers_kernel, n_sub=A_SUBCHUNKS, n_ret=R // B_CHUNK, n_alias=len(aliases)),
        grid=(bsz, n_steps),
        in_specs=in_specs,
        out_specs=[rows(A_WIDTH), sp_spec, sh_spec, rows(B_V_WIDTH), s_spec],
        out_shape=[jax.ShapeDtypeStruct((bsz * n_steps * R, A_WIDTH), BF16), sp_shape, sh_shape,
                   jax.ShapeDtypeStruct((bsz * n_steps * R, B_V_WIDTH), BF16), s_shape],
        scratch_shapes=[pltpu.VMEM((1, A_COLS), F32)],
        input_output_aliases=aliases,
        compiler_params=_params("arbitrary", "arbitrary"),
        name="mixers",
    )(*args)
    return ya, yb, ((sp_all, shift_all), ret_all)


def _merge_kernel(ya_ref, yb_ref, zg_ref, x_ref, gt_ref, g_ref, wa_ref, wb_ref, wo_ref, o_ref):
    rss = _row_halves()
    pas = [jnp.dot(ya_ref[rs, :], wa_ref[...], preferred_element_type=F32) for rs in rss]
    pbs = [jnp.dot(yb_ref[rs, :], wb_ref[...], preferred_element_type=F32) for rs in rss]
    merged = [jax.nn.sigmoid(zg_ref[rs, :D_MODEL].astype(F32)) * pa
              + jax.nn.sigmoid(zg_ref[rs, D_MODEL:].astype(F32)) * pb for rs, pa, pb in zip(rss, pas, pbs)]
    outs = [_dot(m, wo_ref[...]) for m in merged]
    for rs, out in zip(rss, outs):
        o_ref[rs, :] = x_ref[rs, :] + _mod_rows(gt_ref, rs) * _rms(out, g_ref[...])


def _row_halves():
    return [slice(0, ROW_TILE // 2), slice(ROW_TILE // 2, ROW_TILE)]


def _mod_rows(m_ref, rs):
    return m_ref[0] if m_ref.shape[1] == 1 else m_ref[0, rs, :]


def _resident(shape):
    return pl.BlockSpec(shape, lambda i: (0,) * len(shape), pipeline_mode=pl.Buffered(1))


def _merge(ya, yb, zg, x, gt, g, wa, wb, wo, tiles_per_group):
    n, d = x.shape
    rows = lambda w: pl.BlockSpec((ROW_TILE, w), lambda i: (i, 0))
    return pl.pallas_call(
        _merge_kernel,
        grid=(n // ROW_TILE,),
        in_specs=[rows(A_WIDTH), rows(B_V_WIDTH), rows(GATE_COLS), rows(d),
                  _mod_spec(gt, tiles_per_group), _resident((1, d)),
                  _resident(wa.shape), _resident(wb.shape), _resident(wo.shape)],
        out_specs=rows(d),
        out_shape=jax.ShapeDtypeStruct((n, d), F32),
        compiler_params=_params("arbitrary"),
        name="merge",
    )(ya, yb, zg, x, gt, g, wa, wb, wo)


def _ffn_kernel(x_ref, sc_ref, sh_ref, gt_ref, gpre_ref, gpost_ref, up_ref, down_ref, *rest):
    rss = _row_halves()
    xs = [x_ref[rs, :] for rs in rss]
    hs = [(_rms(x, gpre_ref[...]) * (1.0 + _mod_rows(sc_ref, rs)) + _mod_rows(sh_ref, rs)).astype(BF16)
          for x, rs in zip(xs, rss)]
    fs = [jnp.zeros(x.shape, F32) for x in xs]
    for j in range(FFN_HIDDEN // FFN_SLAB):
        sl = slice(j * FFN_SLAB, (j + 1) * FFN_SLAB)
        acts = [jnp.maximum(jnp.dot(h, up_ref[:, sl], preferred_element_type=F32), 0.0) for h in hs]
        fs = [f + _dot(a * a, down_ref[sl, :]) for f, a in zip(fs, acts)]
    for x, f, rs in zip(xs, fs, rss):
        out = x + _mod_rows(gt_ref, rs) * _rms(f, gpost_ref[...])
        if len(rest) == 1:
            (o_ref,) = rest
        else:
            gn_ref, scn_ref, shn_ref, o_ref, hn_ref = rest
            hn_ref[rs, :] = (_rms(out, gn_ref[...]) * (1.0 + _mod_rows(scn_ref, rs))
                             + _mod_rows(shn_ref, rs)).astype(hn_ref.dtype)
        o_ref[rs, :] = out


def _ffn(x, sc, sh, gt, gpre, gpost, up, down, tiles_per_group, nxt):
    n, d = x.shape
    rows = pl.BlockSpec((ROW_TILE, d), lambda i: (i, 0))
    mod = _mod_spec(sc, tiles_per_group)
    in_specs = [rows, mod, mod, mod, _resident((1, d)), _resident((1, d)),
                _resident(up.shape), _resident(down.shape)]
    args = [x, sc, sh, gt, gpre, gpost, up, down]
    out_specs, out_shape = [rows], [jax.ShapeDtypeStruct((n, d), F32)]
    if nxt is not None:
        in_specs += [_resident((1, d)), mod, mod]
        args += list(nxt)
        out_specs.append(rows)
        out_shape.append(jax.ShapeDtypeStruct((n, d), BF16))
    outs = pl.pallas_call(
        _ffn_kernel,
        grid=(n // ROW_TILE,),
        in_specs=in_specs,
        out_specs=out_specs,
        out_shape=out_shape,
        compiler_params=_params("arbitrary"),
        name="ffn",
    )(*args)
    return outs if nxt is not None else (outs[0], None)


def _pad_frames(z, bsz, t, t_pad):
    if t_pad == t:
        return z
    z = z.reshape(bsz, t, -1)
    return jnp.pad(z, ((0, 0), (0, t_pad - t), (0, 0))).reshape(bsz * t_pad, -1)


def _drop_frames(y, bsz, t, t_pad):
    if t_pad == t:
        return y
    return y.reshape(bsz, t_pad, -1)[:, :t].reshape(bsz * t, -1)


def _mods(mod, bsz, t):
    if t % ROW_TILE == 0:
        return [m.reshape(bsz, 1, D_MODEL) for m in jnp.split(mod, 6, axis=-1)], t // ROW_TILE
    n = bsz * t
    return [jnp.repeat(m, t, axis=0).reshape(n // ROW_TILE, ROW_TILE, D_MODEL)
            for m in jnp.split(mod, 6, axis=-1)], 1


def _layer(x, h, mods, nxt, shift0, sp0, ret0, pos0, pr, bsz, t, layer, depth, prev):
    (sh1, sc1, gt1, sh2, sc2, gt2), tiles_per_group = mods
    if h is None:
        h = _prenorm(x, pr["norm_pre0"], sc1, sh1, tiles_per_group)
    za = _matmul(h, pr["w_in_a"], A_COLS, "in_proj_a")
    zb = _matmul(h, pr["w_in_b"], B_COLS // 4, "in_proj_b")
    zg = _matmul(h, pr["w_in_g"], GATE_COLS, "in_proj_g")

    half = B_KEY_DIM // 2
    inv_freq = ROPE_BASE ** (-jnp.linspace(0.0, 1.0, half, dtype=F32))
    ra = A_SUBCHUNKS * A_CHUNK
    if shift0 is None and t % ra == 0:
        ang = (pos0 + jnp.arange(t, dtype=jnp.int32)).astype(F32)[:, None] * inv_freq[None, :]
        ya, yb, states = _mixers(za, zb, jnp.cos(ang), jnp.sin(ang), pr, bsz, t // ra, layer, depth, prev)
    else:
        ta = -(-t // A_CHUNK) * A_CHUNK
        ya, rwkv_states = _rwkv(_pad_frames(za, bsz, t, ta), shift0, sp0, pr, bsz, ta // A_CHUNK, 1,
                                min(t, A_CHUNK), layer, depth, None if prev is None else prev[0])
        ya = _drop_frames(ya, bsz, t, ta)
        tb = -(-t // B_CHUNK) * B_CHUNK
        ang = (pos0 + jnp.arange(tb, dtype=jnp.int32)).astype(F32)[:, None] * inv_freq[None, :]
        yb, ret_states = _ret(_pad_frames(zb, bsz, t, tb), jnp.cos(ang), jnp.sin(ang), ret0, bsz,
                              tb // B_CHUNK, min(t, B_CHUNK), layer, depth, None if prev is None else prev[1])
        yb = _drop_frames(yb, bsz, t, tb)
        states = (rwkv_states, ret_states)

    x = _merge(ya, yb, zg, x, gt1, pr["norm_post0"], pr["w_branch_a"], pr["w_branch_b"], pr["w_out"],
               tiles_per_group)
    x, h_next = _ffn(x, sc2, sh2, gt2, pr["norm_pre1"], pr["norm_post1"], pr["w_ffn_up"], pr["w_ffn_down"],
                     tiles_per_group, nxt)
    return x, h_next, states


def kernel(x_prompt, x_sample, state_shift, state_rwkv, state_ret, c_prompt, c_sample, ada_w, ada_b, norm_pre, norm_post, w_in, a_mu, a_w0, a_w2, a_a0, a_a2, a_g2, a_kk, a_ka, a_rk, a_ln_w, a_ln_b, w_branch_a, w_branch_b, w_out, w_ffn_up, w_ffn_down):
    depth = w_in.shape[0]
    bp, tp, _ = x_prompt.shape
    bs, ts, _ = x_sample.shape
    xp = x_prompt.reshape(bp * tp, D_MODEL)
    xs = x_sample.reshape(bs * ts, D_MODEL)
    c_all = jnp.concatenate([c_prompt, c_sample], axis=0)
    row = lambda a: a.reshape(1, -1)
    lora_pad = jnp.zeros((A_DECAY_RANK, A_WIDTH), BF16)

    mods_p, mods_s = [], []
    for l in range(depth):
        mod = _ada(c_all, ada_w[l].astype(BF16), ada_b[l])
        mods_p.append(_mods(mod[:bp], bp, tp))
        mods_s.append(_mods(mod[bp:], bs, ts))

    def next_inputs(mods, l):
        if l + 1 == depth:
            return None
        (sh1, sc1, _, _, _, _), _ = mods[l + 1]
        return row(norm_pre[l + 1, 0]), sc1, sh1

    prev_p = prev_s = hp = hs = None
    for l in range(depth):
        wl = w_in[l].astype(BF16)
        pr = {
            "norm_pre0": row(norm_pre[l, 0]), "norm_pre1": row(norm_pre[l, 1]),
            "norm_post0": row(norm_post[l, 0]), "norm_post1": row(norm_post[l, 1]),
            "w_in_a": wl[:, :A_COLS], "w_in_b": wl[:, A_COLS:A_COLS + B_COLS],
            "w_in_g": wl[:, A_COLS + B_COLS:],
            "a_mu": row(a_mu[l]), "a_w0": row(a_w0[l]), "a_a0": row(a_a0[l]),
            "a_w2p": jnp.concatenate([a_w2[l].astype(BF16), lora_pad], axis=0),
            "a_a2p": jnp.concatenate([lora_pad, a_a2[l].astype(BF16)], axis=0),
            "a_g2": a_g2[l].astype(BF16),
            "a_kk": row(a_kk[l]), "a_ka": row(a_ka[l]), "a_rk": row(a_rk[l]),
            "a_ln_w": row(a_ln_w[l]), "a_ln_b": row(a_ln_b[l]),
            "w_branch_a": w_branch_a[l].astype(BF16), "w_branch_b": w_branch_b[l].astype(BF16),
            "w_out": w_out[l].astype(BF16),
            "w_ffn_up": w_ffn_up[l].astype(BF16), "w_ffn_down": w_ffn_down[l].astype(BF16),
        }
        xp, hp, prev_p = _layer(xp, hp, mods_p[l], next_inputs(mods_p, l), None, None, None, 0, pr,
                                bp, tp, l, depth, prev_p)
        xs, hs, prev_s = _layer(xs, hs, mods_s[l], next_inputs(mods_s, l),
                                state_shift[l].reshape(bs, 1, A_COLS), _pack_pairs(state_rwkv[l]),
                                state_ret, PAST_LEN, pr, bs, ts, l, depth, prev_s)

    def states(prev, bsz):
        (sp_all, shift_all), ret_all = prev
        return shift_all.reshape(depth, bsz, A_COLS), _unpack_pairs(sp_all), ret_all

    return (xp.reshape(bp, tp, D_MODEL), xs.reshape(bs, ts, D_MODEL), *states(prev_p, bp), *states(prev_s, bs))
```

```python
import functools
import math

import jax
import jax.numpy as jnp
from jax import lax
from jax.experimental import pallas as pl
from jax.experimental.pallas import tpu as pltpu

F32 = jnp.float32
BF16 = jnp.bfloat16

D_MODEL = 1024
A_HEADS = 16
A_HEAD_DIM = 64
A_WIDTH = A_HEADS * A_HEAD_DIM
A_DECAY_RANK = 64
A_ICLR_RANK = 64
A_GATE_RANK = 128
A_COLS = 3 * A_WIDTH + A_DECAY_RANK + A_ICLR_RANK + A_GATE_RANK
A_GN_EPS = 64e-5
B_HEADS = 4
B_KEY_DIM = 256
B_VAL_DIM = 512
B_QK_WIDTH = B_HEADS * B_KEY_DIM
B_V_WIDTH = B_HEADS * B_VAL_DIM
B_COLS = 2 * B_QK_WIDTH + 2 * B_V_WIDTH
GATE_COLS = 2 * D_MODEL
FFN_HIDDEN = 4 * D_MODEL
ROPE_BASE = 10000.0
NORM_EPS = 1e-6
PAST_LEN = 4096

LANES = 128
MXU_DIM = 256
A_PAIRS = A_WIDTH // LANES
A_CHUNK = 64
A_SUBCHUNKS = 4
B_CHUNK = 128
ROW_TILE = 512
FFN_SLAB = 1024
VMEM_LIMIT = 56 * 1024 * 1024


def _dot(a, b):
    return jnp.dot(a.astype(BF16), b.astype(BF16), preferred_element_type=F32)


def _dot_nt(a, b):
    return lax.dot_general(a.astype(BF16), b.astype(BF16), (((1,), (1,)), ((), ())),
                           preferred_element_type=F32)


def _rms(x, g):
    return x * lax.rsqrt(jnp.mean(x * x, axis=-1, keepdims=True) + NORM_EPS) * g


def _params(*sem):
    return pltpu.CompilerParams(dimension_semantics=sem, vmem_limit_bytes=VMEM_LIMIT)


def _ada_kernel(c_ref, w_ref, b_ref, o_ref):
    c = c_ref[...]
    o_ref[...] = _dot(c * jax.nn.sigmoid(c), w_ref[...]) + b_ref[...]


def _ada(c, w, b):
    n, d = c.shape
    cols = w.shape[1]
    tn = cols // 4
    return pl.pallas_call(
        _ada_kernel,
        grid=(cols // tn,),
        in_specs=[pl.BlockSpec((n, d), lambda j: (0, 0)),
                  pl.BlockSpec((d, tn), lambda j: (0, j)),
                  pl.BlockSpec((1, tn), lambda j: (0, j))],
        out_specs=pl.BlockSpec((n, tn), lambda j: (0, j)),
        out_shape=jax.ShapeDtypeStruct((n, cols), F32),
        compiler_params=_params("arbitrary"),
        name="ada",
    )(c, w, b.reshape(1, cols))


def _prenorm_kernel(x_ref, g_ref, sc_ref, sh_ref, o_ref):
    h = _rms(x_ref[...], g_ref[...]) * (1.0 + sc_ref[0]) + sh_ref[0]
    o_ref[...] = h.astype(o_ref.dtype)


def _mod_spec(mod, tiles_per_group):
    _, r, d = mod.shape
    return pl.BlockSpec((1, r, d), lambda i: (i // tiles_per_group, 0, 0))


def _prenorm(x, g, sc, sh, tiles_per_group):
    n, d = x.shape
    return pl.pallas_call(
        _prenorm_kernel,
        grid=(n // ROW_TILE,),
        in_specs=[pl.BlockSpec((ROW_TILE, d), lambda i: (i, 0)),
                  pl.BlockSpec((1, d), lambda i: (0, 0)),
                  _mod_spec(sc, tiles_per_group), _mod_spec(sh, tiles_per_group)],
        out_specs=pl.BlockSpec((ROW_TILE, d), lambda i: (i, 0)),
        out_shape=jax.ShapeDtypeStruct((n, d), BF16),
        compiler_params=_params("arbitrary"),
        name="prenorm",
    )(x, g, sc, sh)


def _matmul_kernel(a_ref, w_ref, o_ref):
    o_ref[...] = jnp.dot(a_ref[...], w_ref[...], preferred_element_type=F32).astype(o_ref.dtype)


def _matmul(a, w, tn, name):
    n, k = a.shape
    cols = w.shape[1]
    tm = min(n, 1024)
    return pl.pallas_call(
        _matmul_kernel,
        grid=(cols // tn, n // tm),
        in_specs=[pl.BlockSpec((tm, k), lambda j, i: (i, 0)),
                  pl.BlockSpec((k, tn), lambda j, i: (0, j))],
        out_specs=pl.BlockSpec((tm, tn), lambda j, i: (i, j)),
        out_shape=jax.ShapeDtypeStruct((n, cols), BF16),
        compiler_params=_params("arbitrary", "arbitrary"),
        name=name,
    )(a, w)


def _head_sum(x, ones_bd):
    c = x.shape[0]
    groups = A_WIDTH // MXU_DIM
    xs = jnp.concatenate([x[:, q * MXU_DIM:(q + 1) * MXU_DIM] for q in range(groups)], axis=0)
    s = jnp.dot(xs.astype(BF16), ones_bd, preferred_element_type=F32)
    return jnp.concatenate([s[q * c:(q + 1) * c] for q in range(groups)], axis=1)


def _rwkv_init(shift0_ref, sp0_ref, sp_ref, carry_ref):
    if shift0_ref is None:
        carry_ref[...] = jnp.zeros(carry_ref.shape, F32)
        sp_ref[...] = jnp.zeros(sp_ref.shape, F32)
    else:
        carry_ref[...] = shift0_ref[0]
        sp_ref[...] = sp0_ref[0]


def _rwkv_kernel(*refs, valid_rows, n_sub, zero_init, n_alias):
    shift0_ref = sp0_ref = None
    if zero_init:
        za_ref = refs[0]
        refs = refs[1:]
    else:
        za_ref, shift0_ref, sp0_ref = refs[:3]
        refs = refs[3:]
    ya_ref, sp_ref, shift_out_ref, carry_ref = refs[11 + n_alias:]

    @pl.when(pl.program_id(1) == 0)
    def _():
        _rwkv_init(shift0_ref, sp0_ref, sp_ref, carry_ref)

    _rwkv_main(za_ref, refs[:11], ya_ref, sp_ref, shift_out_ref, carry_ref, valid_rows, n_sub)


def _rwkv_main(za_ref, prm, ya_ref, sp_ref, shift_out_ref, carry_ref, valid_rows, n_sub):
    mu_ref, w0_ref, w2_ref, a0_ref, a2_ref, g2_ref, kkp_ref, kap_ref, rk_ref, lnw_ref, lnb_ref = prm
    C = A_CHUNK
    C2 = 2 * C
    R = n_sub * C

    za = za_ref[...].astype(F32)
    row = lax.broadcasted_iota(jnp.int32, (R, 1), 0)
    za_prev = jnp.where(row == 0, carry_ref[...], pltpu.roll(za, 1, axis=0))
    carry_ref[...] = za[R - 1:R, :]
    last = R - C + valid_rows - 1
    shift_out_ref[...] = za[last:last + 1, :]
    mix = za + (za_prev - za) * mu_ref[...]

    r = mix[:, 0:A_WIDTH]
    k = mix[:, A_WIDTH:2 * A_WIDTH]
    v = mix[:, 2 * A_WIDTH:3 * A_WIDTH]
    zwi = mix[:, 3 * A_WIDTH:3 * A_WIDTH + LANES]
    zg = mix[:, 3 * A_WIDTH + LANES:A_COLS]

    lw = -math.exp(-0.5) * jax.nn.sigmoid(w0_ref[...] + _dot(jnp.tanh(zwi), w2_ref[...]))
    iclr = jax.nn.sigmoid(a0_ref[...] + _dot(zwi, a2_ref[...]))
    gate = _dot(jax.nn.sigmoid(zg), g2_ref[...])

    li = lax.broadcasted_iota(jnp.int32, (LANES, LANES), 0)
    lj = lax.broadcasted_iota(jnp.int32, (LANES, LANES), 1)
    blk_diag = (li // A_HEAD_DIM) == (lj // A_HEAD_DIM)
    mi = lax.broadcasted_iota(jnp.int32, (MXU_DIM, MXU_DIM), 0)
    mj = lax.broadcasted_iota(jnp.int32, (MXU_DIM, MXU_DIM), 1)
    ones_bd = jnp.where((mi // A_HEAD_DIM) == (mj // A_HEAD_DIM), 1.0, 0.0).astype(BF16)

    kk = k * kkp_ref[...]
    kk = kk * lax.rsqrt(jnp.maximum(_head_sum(kk * kk, ones_bd), 1e-24))
    k2 = k * (1.0 + (iclr - 1.0) * kap_ref[...])
    if valid_rows < C:
        valid = row < valid_rows
        lw = jnp.where(valid, lw, 0.0)
        kk = jnp.where(valid, kk, 0.0)
        k2 = jnp.where(valid, k2, 0.0)
    b = kk * iclr

    cum = lw
    shift = 1
    while shift < C:
        cum = cum + jnp.where((row % C) >= shift, pltpu.roll(cum, shift, axis=0), 0.0)
        shift *= 2
    e_p = jnp.exp(cum)
    e_m = jnp.exp(-cum)
    e_tots = [e_p[j * C + C - 1:j * C + C, :] for j in range(n_sub)]
    e_tot_rows = jnp.concatenate([jnp.broadcast_to(e, (C, A_WIDTH)) for e in e_tots], axis=0)
    rt = r * e_p
    at = -kk * jnp.exp(cum - lw)
    bt = b * e_m
    kt = k2 * e_m
    kh = kt * e_tot_rows
    bh = bt * e_tot_rows

    gi = lax.broadcasted_iota(jnp.int32, (2 * C2, C2), 0)
    gj = lax.broadcasted_iota(jnp.int32, (2 * C2, C2), 1)
    blk = gi // C
    lower = (gi % C) >= (gj % C)
    mask_g = lower & (((blk == 1) | (blk == 2)) | ((gi % C) != (gj % C)))
    qi = lax.broadcasted_iota(jnp.int32, (C2, C2), 0)
    qj = lax.broadcasted_iota(jnp.int32, (C2, C2), 1)
    blk_tl = (qi < C) & (qj < C)
    blk_br = (qi >= C) & (qj >= C)
    eye2 = jnp.where(qi == qj, 1.0, 0.0).astype(F32)
    lane = lax.broadcasted_iota(jnp.int32, (1, LANES), 1)
    m0 = lane < A_HEAD_DIM
    zeros = jnp.zeros((C, LANES), F32)
    n_double = int(math.log2(C)) - 1

    pairs = range(A_PAIRS)
    lanes = [slice(p * LANES, (p + 1) * LANES) for p in pairs]
    rows = [slice(j * C, (j + 1) * C) for j in range(n_sub)]
    streams = [(rw, ln) for rw in rows for ln in lanes]
    cat = lambda *xs: jnp.concatenate(xs, axis=0)
    only0 = lambda x: jnp.where(m0, x, 0.0)
    only1 = lambda x: jnp.where(m0, 0.0, x)
    split01 = lambda x: jnp.where(m0, x[0:C], x[C:C2])
    H = C // 2
    zeros_h = jnp.zeros((H, C2), F32)
    later = lambda x: cat(x[H:C], x[C + H:C2])
    spread = lambda x: cat(zeros_h, x[0:H], zeros_h, x[H:C])
    def state_free(st):
        gs = [jnp.where(mask_g, _dot(cat(only0(at[s]), only0(rt[s]), only1(rt[s]), only1(at[s])),
                                     cat(bt[s], kt[s]).T), 0.0) for s in st]
        pws = [jnp.where(blk_tl, g[0:C2], jnp.where(blk_br, pltpu.roll(g[C2:2 * C2], C, axis=1), 0.0))
               for g in gs]
        tts = [eye2 + nn for nn in pws]
        pws = [_dot(pw, pw) for pw in pws]
        for _ in range(n_double - 2):
            both = [_dot(cat(pw, tt), pw) for pw, tt in zip(pws, tts)]
            pws = [bth[0:C2] for bth in both]
            tts = [tt + bth[C2:2 * C2] for tt, bth in zip(tts, both)]
        both = [_dot(cat(later(pw), tt), pw) for pw, tt in zip(pws, tts)]
        pws = [bth[0:C] for bth in both]
        tts = [tt + bth[C:C + C2] for tt, bth in zip(tts, both)]
        tts = [tt + spread(_dot(later(tt), spread(pw))) for tt, pw in zip(tts, pws)]
        t01s = [tt[0:C] + tt[C:C2] for tt in tts]
        akv = [split01(_dot(cat(g[0:C], g[3 * C:4 * C]), cat(zeros, v[s]))) for g, s in zip(gs, st)]
        return gs, t01s, akv

    def with_state(j, st, free, sps):
        gs, t01s, akv = free
        from_state = [_dot(cat(at[s], rt[s]), sp.T) for s, sp in zip(st, sps)]
        ws = [fs[0:C] + x for fs, x in zip(from_state, akv)]
        us = [_dot(t01, cat(only0(w), only1(w))) for t01, w in zip(t01s, ws)]
        ys = [fs[C:C2] + split01(_dot(g[C:3 * C], cat(u, v[s])))
              for fs, g, s, u in zip(from_state, gs, st, us)]
        sps = [jnp.where(blk_diag, sp * e_tots[j][:, s[1]] + _dot(cat(v[s], u).T, cat(kh[s], bh[s])), 0.0)
               for s, sp, u in zip(st, sps, us)]
        return ys, sps

    chunk_streams = [streams[j * A_PAIRS:(j + 1) * A_PAIRS] for j in range(n_sub)]
    sps = [sp_ref[p] for p in pairs]
    ys = []
    free = state_free(chunk_streams[0])
    for j in range(n_sub):
        nxt = state_free(chunk_streams[j + 1]) if j + 1 < n_sub else None
        yj, sps = with_state(j, chunk_streams[j], free, sps)
        ys.append(yj)
        free = nxt
    for p in pairs:
        sp_ref[p] = sps[p]

    y = jnp.concatenate([jnp.concatenate(yj, axis=1) for yj in ys], axis=0)
    mu = _head_sum(y, ones_bd) * (1.0 / A_HEAD_DIM)
    d = y - mu
    var = _head_sum(d * d, ones_bd) * (1.0 / A_HEAD_DIM)
    yn = d * lax.rsqrt(var + A_GN_EPS) * lnw_ref[...] + lnb_ref[...]
    bonus = _head_sum(r * k2 * rk_ref[...], ones_bd) * v
    ya_ref[...] = ((yn + bonus) * gate).astype(ya_ref.dtype)


def _stacked_out(layer, depth, bsz, tail):
    spec = pl.BlockSpec((None, None) + tail, lambda b, c: (layer, b) + (0,) * len(tail))
    return spec, jax.ShapeDtypeStruct((depth, bsz) + tail, F32)


def _rwkv(za, shift0, sp0, pr, bsz, n_steps, n_sub, valid_rows, layer, depth, prev):
    R = n_sub * A_CHUNK
    zero_init = shift0 is None
    vec = lambda n: pl.BlockSpec((1, n), lambda b, c: (0, 0))
    mat = lambda r, n: pl.BlockSpec((r, n), lambda b, c: (0, 0))
    in_specs = [pl.BlockSpec((R, A_COLS), lambda b, c: (b * n_steps + c, 0))]
    args = [za]
    if not zero_init:
        in_specs += [pl.BlockSpec((1, 1, A_COLS), lambda b, c: (b, 0, 0)),
                     pl.BlockSpec((1, A_PAIRS, LANES, LANES), lambda b, c: (b, 0, 0, 0))]
        args += [shift0, sp0]
    in_specs += [vec(A_COLS), vec(A_WIDTH), mat(LANES, A_WIDTH), vec(A_WIDTH), mat(LANES, A_WIDTH),
                 mat(A_GATE_RANK, A_WIDTH), vec(A_WIDTH), vec(A_WIDTH), vec(A_WIDTH),
                 vec(A_WIDTH), vec(A_WIDTH)]
    args += [pr["a_mu"], pr["a_w0"], pr["a_w2p"], pr["a_a0"], pr["a_a2p"], pr["a_g2"],
             pr["a_kk"], pr["a_ka"], pr["a_rk"], pr["a_ln_w"], pr["a_ln_b"]]
    aliases = {}
    if prev is not None:
        aliases = {len(args): 1, len(args) + 1: 2}
        in_specs += [pl.BlockSpec(memory_space=pl.ANY)] * 2
        args += list(prev)
    sp_spec, sp_shape = _stacked_out(layer, depth, bsz, (A_PAIRS, LANES, LANES))
    sh_spec, sh_shape = _stacked_out(layer, depth, bsz, (1, A_COLS))
    ya, sp_all, shift_all = pl.pallas_call(
        functools.partial(_rwkv_kernel, valid_rows=valid_rows, n_sub=n_sub, zero_init=zero_init,
                          n_alias=len(aliases)),
        grid=(bsz, n_steps),
        in_specs=in_specs,
        out_specs=[pl.BlockSpec((R, A_WIDTH), lambda b, c: (b * n_steps + c, 0)), sp_spec, sh_spec],
        out_shape=[jax.ShapeDtypeStruct((bsz * n_steps * R, A_WIDTH), BF16), sp_shape, sh_shape],
        scratch_shapes=[pltpu.VMEM((1, A_COLS), F32)],
        input_output_aliases=aliases,
        compiler_params=_params("arbitrary", "arbitrary"),
        name="rwkv",
    )(*args)
    return ya, (sp_all, shift_all)


def _pack_pairs(s):
    bsz = s.shape[0]
    s = s.reshape(bsz, A_PAIRS, 2, A_HEAD_DIM, A_HEAD_DIM)
    z = jnp.zeros_like(s[:, :, 0])
    top = jnp.concatenate([s[:, :, 0], z], axis=-1)
    bot = jnp.concatenate([z, s[:, :, 1]], axis=-1)
    return jnp.concatenate([top, bot], axis=-2)


def _unpack_pairs(sp):
    h = A_HEAD_DIM
    s = jnp.stack([sp[..., :h, :h], sp[..., h:, h:]], axis=-3)
    return s.reshape(sp.shape[:-3] + (A_HEADS, h, h))


def _ret_kernel(*refs, valid_rows, zero_init, n_alias):
    s0_ref = None
    if zero_init:
        zb_ref, cos_ref, sin_ref = refs[:3]
    else:
        zb_ref, cos_ref, sin_ref, s0_ref = refs[:4]
    yb_ref, s_ref = refs[-2:]

    @pl.when(pl.program_id(1) == 0)
    def _():
        _ret_init(s0_ref, s_ref)

    _ret_main(zb_ref, cos_ref, sin_ref, yb_ref, s_ref, valid_rows, 1)


def _ret_init(s0_ref, s_ref):
    if s0_ref is None:
        s_ref[...] = jnp.zeros(s_ref.shape, F32)
    else:
        s_ref[...] = s0_ref[0]


def _ret_main(zb_ref, cos_ref, sin_ref, yb_ref, s_ref, valid_rows, n_chunks):
    C = B_CHUNK
    half = B_KEY_DIM // 2
    ri = lax.broadcasted_iota(jnp.int32, (C, C), 0)
    rj = lax.broadcasted_iota(jnp.int32, (C, C), 1)
    diff = (ri - rj).astype(F32)
    idx = lax.broadcasted_iota(jnp.int32, (C, 1), 0).astype(F32)

    for i in range(n_chunks):
        rs = slice(i * C, (i + 1) * C)
        cos = cos_ref[rs, :]
        sin = sin_ref[rs, :]

        def rope(x):
            x1, x2 = x[:, :half], x[:, half:]
            return jnp.concatenate([x1 * cos - x2 * sin, x1 * sin + x2 * cos], axis=1)

        for h in range(B_HEADS):
            log_g = math.log(1.0 - 2.0 ** (-5.0 - h))
            q = rope(zb_ref[rs, h * B_KEY_DIM:(h + 1) * B_KEY_DIM].astype(F32))
            k = rope(zb_ref[rs, B_QK_WIDTH + h * B_KEY_DIM:B_QK_WIDTH + (h + 1) * B_KEY_DIM].astype(F32))
            k = k * (B_KEY_DIM ** -0.5)
            v = zb_ref[rs, 2 * B_QK_WIDTH + h * B_VAL_DIM:2 * B_QK_WIDTH + (h + 1) * B_VAL_DIM]
            g = zb_ref[rs, 2 * B_QK_WIDTH + B_V_WIDTH + h * B_VAL_DIM:
                       2 * B_QK_WIDTH + B_V_WIDTH + (h + 1) * B_VAL_DIM].astype(F32)
            mask = jnp.where(diff >= 0.0, jnp.exp(jnp.maximum(diff, 0.0) * log_g), 0.0)
            scores = _dot_nt(q, k) * mask
            s = s_ref[h]
            y = _dot(scores, v) + _dot(q * jnp.exp((idx + 1.0) * log_g), s)
            k_dec = jnp.where(idx < valid_rows, k * jnp.exp((valid_rows - 1.0 - idx) * log_g), 0.0)
            s_ref[h] = s * math.exp(valid_rows * log_g) + _dot(k_dec.T, v)
            yn = y * lax.rsqrt(jnp.mean(y * y, axis=-1, keepdims=True) + NORM_EPS)
            yb_ref[rs, h * B_VAL_DIM:(h + 1) * B_VAL_DIM] = (g * jax.nn.sigmoid(g) * yn).astype(yb_ref.dtype)


def _ret(zb, cos, sin, s0, bsz, n_chunks, valid_rows, layer, depth, prev):
    C = B_CHUNK
    half = B_KEY_DIM // 2
    zero_init = s0 is None
    in_specs = [pl.BlockSpec((C, B_COLS), lambda b, c: (b * n_chunks + c, 0)),
                pl.BlockSpec((C, half), lambda b, c: (c, 0)),
                pl.BlockSpec((C, half), lambda b, c: (c, 0))]
    args = [zb, cos, sin]
    if not zero_init:
        in_specs.append(pl.BlockSpec((None, 1, B_HEADS, B_KEY_DIM, B_VAL_DIM),
                                     lambda b, c: (layer, b, 0, 0, 0)))
        args.append(s0)
    aliases = {}
    if prev is not None:
        aliases = {len(args): 1}
        in_specs.append(pl.BlockSpec(memory_space=pl.ANY))
        args.append(prev)
    s_spec, s_shape = _stacked_out(layer, depth, bsz, (B_HEADS, B_KEY_DIM, B_VAL_DIM))
    return pl.pallas_call(
        functools.partial(_ret_kernel, valid_rows=valid_rows, zero_init=zero_init, n_alias=len(aliases)),
        grid=(bsz, n_chunks),
        in_specs=in_specs,
        out_specs=[pl.BlockSpec((C, B_V_WIDTH), lambda b, c: (b * n_chunks + c, 0)), s_spec],
        out_shape=[jax.ShapeDtypeStruct((bsz * n_chunks * C, B_V_WIDTH), BF16), s_shape],
        input_output_aliases=aliases,
        compiler_params=_params("arbitrary", "arbitrary"),
        name="ret",
    )(*args)


def _mixers_kernel(*refs, n_sub, n_ret, n_alias):
    za_ref, zb_ref, cos_ref, sin_ref = refs[:4]
    prm = refs[4:15]
    ya_ref, sp_ref, shift_out_ref, yb_ref, s_ref, carry_ref = refs[15 + n_alias:]

    @pl.when(pl.program_id(1) == 0)
    def _():
        _rwkv_init(None, None, sp_ref, carry_ref)
        _ret_init(None, s_ref)

    _rwkv_main(za_ref, prm, ya_ref, sp_ref, shift_out_ref, carry_ref, A_CHUNK, n_sub)
    _ret_main(zb_ref, cos_ref, sin_ref, yb_ref, s_ref, B_CHUNK, n_ret)


def _mixers(za, zb, cos, sin, pr, bsz, n_steps, layer, depth, prev):
    R = A_SUBCHUNKS * A_CHUNK
    half = B_KEY_DIM // 2
    rows = lambda w: pl.BlockSpec((R, w), lambda b, c: (b * n_steps + c, 0))
    vec = lambda n: pl.BlockSpec((1, n), lambda b, c: (0, 0))
    mat = lambda r, n: pl.BlockSpec((r, n), lambda b, c: (0, 0))
    in_specs = [rows(A_COLS), rows(B_COLS),
                pl.BlockSpec((R, half), lambda b, c: (c, 0)), pl.BlockSpec((R, half), lambda b, c: (c, 0)),
                vec(A_COLS), vec(A_WIDTH), mat(LANES, A_WIDTH), vec(A_WIDTH), mat(LANES, A_WIDTH),
                mat(A_GATE_RANK, A_WIDTH), vec(A_WIDTH), vec(A_WIDTH), vec(A_WIDTH),
                vec(A_WIDTH), vec(A_WIDTH)]
    args = [za, zb, cos, sin, pr["a_mu"], pr["a_w0"], pr["a_w2p"], pr["a_a0"], pr["a_a2p"], pr["a_g2"],
            pr["a_kk"], pr["a_ka"], pr["a_rk"], pr["a_ln_w"], pr["a_ln_b"]]
    aliases = {}
    if prev is not None:
        (sp_prev, shift_prev), ret_prev = prev
        aliases = {len(args): 1, len(args) + 1: 2, len(args) + 2: 4}
        in_specs += [pl.BlockSpec(memory_space=pl.ANY)] * 3
        args += [sp_prev, shift_prev, ret_prev]
    sp_spec, sp_shape = _stacked_out(layer, depth, bsz, (A_PAIRS, LANES, LANES))
    sh_spec, sh_shape = _stacked_out(layer, depth, bsz, (1, A_COLS))
    s_spec, s_shape = _stacked_out(layer, depth, bsz, (B_HEADS, B_KEY_DIM, B_VAL_DIM))
    ya, sp_all, shift_all, yb, ret_all = pl.pallas_call(
        functools.partial(_mixers_kernel, n_sub=A_SUBCHUNKS, n_ret=R // B_CHUNK, n_alias=len(aliases)),
        grid=(bsz, n_steps),
        in_specs=in_specs,
        out_specs=[rows(A_WIDTH), sp_spec, sh_spec, rows(B_V_WIDTH), s_spec],
        out_shape=[jax.ShapeDtypeStruct((bsz * n_steps * R, A_WIDTH), BF16), sp_shape, sh_shape,
                   jax.ShapeDtypeStruct((bsz * n_steps * R, B_V_WIDTH), BF16), s_shape],
        scratch_shapes=[pltpu.VMEM((1, A_COLS), F32)],
        input_output_aliases=aliases,
        compiler_params=_params("arbitrary", "arbitrary"),
        name="mixers",
    )(*args)
    return ya, yb, ((sp_all, shift_all), ret_all)


def _merge_kernel(ya_ref, yb_ref, zg_ref, x_ref, gt_ref, g_ref, wa_ref, wb_ref, wo_ref, o_ref):
    rss = _row_halves()
    pas = [jnp.dot(ya_ref[rs, :], wa_ref[...], preferred_element_type=F32) for rs in rss]
    pbs = [jnp.dot(yb_ref[rs, :], wb_ref[...], preferred_element_type=F32) for rs in rss]
    merged = [jax.nn.sigmoid(zg_ref[rs, :D_MODEL].astype(F32)) * pa
              + jax.nn.sigmoid(zg_ref[rs, D_MODEL:].astype(F32)) * pb for rs, pa, pb in zip(rss, pas, pbs)]
    outs = [_dot(m, wo_ref[...]) for m in merged]
    for rs, out in zip(rss, outs):
        o_ref[rs, :] = x_ref[rs, :] + _mod_rows(gt_ref, rs) * _rms(out, g_ref[...])


def _row_halves():
    return [slice(0, ROW_TILE // 2), slice(ROW_TILE // 2, ROW_TILE)]


def _mod_rows(m_ref, rs):
    return m_ref[0] if m_ref.shape[1] == 1 else m_ref[0, rs, :]


def _resident(shape):
    return pl.BlockSpec(shape, lambda i: (0,) * len(shape), pipeline_mode=pl.Buffered(1))


def _merge(ya, yb, zg, x, gt, g, wa, wb, wo, tiles_per_group):
    n, d = x.shape
    rows = lambda w: pl.BlockSpec((ROW_TILE, w), lambda i: (i, 0))
    return pl.pallas_call(
        _merge_kernel,
        grid=(n // ROW_TILE,),
        in_specs=[rows(A_WIDTH), rows(B_V_WIDTH), rows(GATE_COLS), rows(d),
                  _mod_spec(gt, tiles_per_group), _resident((1, d)),
                  _resident(wa.shape), _resident(wb.shape), _resident(wo.shape)],
        out_specs=rows(d),
        out_shape=jax.ShapeDtypeStruct((n, d), F32),
        compiler_params=_params("arbitrary"),
        name="merge",
    )(ya, yb, zg, x, gt, g, wa, wb, wo)


def _ffn_kernel(x_ref, sc_ref, sh_ref, gt_ref, gpre_ref, gpost_ref, up_ref, down_ref, *rest):
    rss = _row_halves()
    xs = [x_ref[rs, :] for rs in rss]
    hs = [(_rms(x, gpre_ref[...]) * (1.0 + _mod_rows(sc_ref, rs)) + _mod_rows(sh_ref, rs)).astype(BF16)
          for x, rs in zip(xs, rss)]
    fs = [jnp.zeros(x.shape, F32) for x in xs]
    for j in range(FFN_HIDDEN // FFN_SLAB):
        sl = slice(j * FFN_SLAB, (j + 1) * FFN_SLAB)
        acts = [jnp.maximum(jnp.dot(h, up_ref[:, sl], preferred_element_type=F32), 0.0) for h in hs]
        fs = [f + _dot(a * a, down_ref[sl, :]) for f, a in zip(fs, acts)]
    for x, f, rs in zip(xs, fs, rss):
        out = x + _mod_rows(gt_ref, rs) * _rms(f, gpost_ref[...])
        if len(rest) == 1:
            (o_ref,) = rest
        else:
            gn_ref, scn_ref, shn_ref, o_ref, hn_ref = rest
            hn_ref[rs, :] = (_rms(out, gn_ref[...]) * (1.0 + _mod_rows(scn_ref, rs))
                             + _mod_rows(shn_ref, rs)).astype(hn_ref.dtype)
        o_ref[rs, :] = out


def _ffn(x, sc, sh, gt, gpre, gpost, up, down, tiles_per_group, nxt):
    n, d = x.shape
    rows = pl.BlockSpec((ROW_TILE, d), lambda i: (i, 0))
    mod = _mod_spec(sc, tiles_per_group)
    in_specs = [rows, mod, mod, mod, _resident((1, d)), _resident((1, d)),
                _resident(up.shape), _resident(down.shape)]
    args = [x, sc, sh, gt, gpre, gpost, up, down]
    out_specs, out_shape = [rows], [jax.ShapeDtypeStruct((n, d), F32)]
    if nxt is not None:
        in_specs += [_resident((1, d)), mod, mod]
        args += list(nxt)
        out_specs.append(rows)
        out_shape.append(jax.ShapeDtypeStruct((n, d), BF16))
    outs = pl.pallas_call(
        _ffn_kernel,
        grid=(n // ROW_TILE,),
        in_specs=in_specs,
        out_specs=out_specs,
        out_shape=out_shape,
        compiler_params=_params("arbitrary"),
        name="ffn",
    )(*args)
    return outs if nxt is not None else (outs[0], None)


def _pad_frames(z, bsz, t, t_pad):
    if t_pad == t:
        return z
    z = z.reshape(bsz, t, -1)
    return jnp.pad(z, ((0, 0), (0, t_pad - t), (0, 0))).reshape(bsz * t_pad, -1)


def _drop_frames(y, bsz, t, t_pad):
    if t_pad == t:
        return y
    return y.reshape(bsz, t_pad, -1)[:, :t].reshape(bsz * t, -1)


def _mods(mod, bsz, t):
    if t % ROW_TILE == 0:
        return [m.reshape(bsz, 1, D_MODEL) for m in jnp.split(mod, 6, axis=-1)], t // ROW_TILE
    n = bsz * t
    return [jnp.repeat(m, t, axis=0).reshape(n // ROW_TILE, ROW_TILE, D_MODEL)
            for m in jnp.split(mod, 6, axis=-1)], 1


def _layer(x, h, mods, nxt, shift0, sp0, ret0, pos0, pr, bsz, t, layer, depth, prev):
    (sh1, sc1, gt1, sh2, sc2, gt2), tiles_per_group = mods
    if h is None:
        h = _prenorm(x, pr["norm_pre0"], sc1, sh1, tiles_per_group)
    za = _matmul(h, pr["w_in_a"], A_COLS, "in_proj_a")
    zb = _matmul(h, pr["w_in_b"], B_COLS // 4, "in_proj_b")
    zg = _matmul(h, pr["w_in_g"], GATE_COLS, "in_proj_g")

    half = B_KEY_DIM // 2
    inv_freq = ROPE_BASE ** (-jnp.linspace(0.0, 1.0, half, dtype=F32))
    ra = A_SUBCHUNKS * A_CHUNK
    if shift0 is None and t % ra == 0:
        ang = (pos0 + jnp.arange(t, dtype=jnp.int32)).astype(F32)[:, None] * inv_freq[None, :]
        ya, yb, states = _mixers(za, zb, jnp.cos(ang), jnp.sin(ang), pr, bsz, t // ra, layer, depth, prev)
    else:
        ta = -(-t // A_CHUNK) * A_CHUNK
        ya, rwkv_states = _rwkv(_pad_frames(za, bsz, t, ta), shift0, sp0, pr, bsz, ta // A_CHUNK, 1,
                                min(t, A_CHUNK), layer, depth, None if prev is None else prev[0])
        ya = _drop_frames(ya, bsz, t, ta)
        tb = -(-t // B_CHUNK) * B_CHUNK
        ang = (pos0 + jnp.arange(tb, dtype=jnp.int32)).astype(F32)[:, None] * inv_freq[None, :]
        yb, ret_states = _ret(_pad_frames(zb, bsz, t, tb), jnp.cos(ang), jnp.sin(ang), ret0, bsz,
                              tb // B_CHUNK, min(t, B_CHUNK), layer, depth, None if prev is None else prev[1])
        yb = _drop_frames(yb, bsz, t, tb)
        states = (rwkv_states, ret_states)

    x = _merge(ya, yb, zg, x, gt1, pr["norm_post0"], pr["w_branch_a"], pr["w_branch_b"], pr["w_out"],
               tiles_per_group)
    x, h_next = _ffn(x, sc2, sh2, gt2, pr["norm_pre1"], pr["norm_post1"], pr["w_ffn_up"], pr["w_ffn_down"],
                     tiles_per_group, nxt)
    return x, h_next, states


def kernel(x_prompt, x_sample, state_shift, state_rwkv, state_ret, c_prompt, c_sample, ada_w, ada_b, norm_pre, norm_post, w_in, a_mu, a_w0, a_w2, a_a0, a_a2, a_g2, a_kk, a_ka, a_rk, a_ln_w, a_ln_b, w_branch_a, w_branch_b, w_out, w_ffn_up, w_ffn_down):
    depth = w_in.shape[0]
    bp, tp, _ = x_prompt.shape
    bs, ts, _ = x_sample.shape
    xp = x_prompt.reshape(bp * tp, D_MODEL)
    xs = x_sample.reshape(bs * ts, D_MODEL)
    c_all = jnp.concatenate([c_prompt, c_sample], axis=0)
    row = lambda a: a.reshape(1, -1)
    lora_pad = jnp.zeros((A_DECAY_RANK, A_WIDTH), BF16)

    mods_p, mods_s = [], []
    for l in range(depth):
        mod = _ada(c_all, ada_w[l].astype(BF16), ada_b[l])
        mods_p.append(_mods(mod[:bp], bp, tp))
        mods_s.append(_mods(mod[bp:], bs, ts))

    def next_inputs(mods, l):
        if l + 1 == depth:
            return None
        (sh1, sc1, _, _, _, _), _ = mods[l + 1]
        return row(norm_pre[l + 1, 0]), sc1, sh1

    prev_p = prev_s = hp = hs = None
    for l in range(depth):
        wl = w_in[l].astype(BF16)
        pr = {
            "norm_pre0": row(norm_pre[l, 0]), "norm_pre1": row(norm_pre[l, 1]),
            "norm_post0": row(norm_post[l, 0]), "norm_post1": row(norm_post[l, 1]),
            "w_in_a": wl[:, :A_COLS], "w_in_b": wl[:, A_COLS:A_COLS + B_COLS],
            "w_in_g": wl[:, A_COLS + B_COLS:],
            "a_mu": row(a_mu[l]), "a_w0": row(a_w0[l]), "a_a0": row(a_a0[l]),
            "a_w2p": jnp.concatenate([a_w2[l].astype(BF16), lora_pad], axis=0),
            "a_a2p": jnp.concatenate([lora_pad, a_a2[l].astype(BF16)], axis=0),
            "a_g2": a_g2[l].astype(BF16),
            "a_kk": row(a_kk[l]), "a_ka": row(a_ka[l]), "a_rk": row(a_rk[l]),
            "a_ln_w": row(a_ln_w[l]), "a_ln_b": row(a_ln_b[l]),
            "w_branch_a": w_branch_a[l].astype(BF16), "w_branch_b": w_branch_b[l].astype(BF16),
            "w_out": w_out[l].astype(BF16),
            "w_ffn_up": w_ffn_up[l].astype(BF16), "w_ffn_down": w_ffn_down[l].astype(BF16),
        }
        xp, hp, prev_p = _layer(xp, hp, mods_p[l], next_inputs(mods_p, l), None, None, None, 0, pr,
                                bp, tp, l, depth, prev_p)
        xs, hs, prev_s = _layer(xs, hs, mods_s[l], next_inputs(mods_s, l),
                                state_shift[l].reshape(bs, 1, A_COLS), _pack_pairs(state_rwkv[l]),
                                state_ret, PAST_LEN, pr, bs, ts, l, depth, prev_s)

    def states(prev, bsz):
        (sp_all, shift_all), ret_all = prev
        return shift_all.reshape(depth, bsz, A_COLS), _unpack_pairs(sp_all), ret_all

    return (xp.reshape(bp, tp, D_MODEL), xs.reshape(bs, ts, D_MODEL), *states(prev_p, bp), *states(prev_s, bs))
```
